```python
import math
import jax, jax.numpy as jnp
from jax import lax
import numpy as np

D_MODEL = 2048
BATCH = 4
SEQ = 4096
DEPTH = 2

GRID_W = 64
CTX_LEN = 256
CHUNK = 128
A_WIDTH = D_MODEL // 2
A_GROUPS = 8
A_GROUP_DIM = A_WIDTH // A_GROUPS
A_COLS = 2 * A_WIDTH
B_WIDTH = D_MODEL - A_WIDTH
HEAD_DIM = 64
B_HEADS = B_WIDTH // HEAD_DIM
W_LORA = max(32, int(round(1.8 * B_WIDTH ** 0.5 / 32)) * 32)
A_LORA = max(32, int(round(1.8 * B_WIDTH ** 0.5 / 32)) * 32)
G_LORA = max(32, int(round(0.6 * B_WIDTH ** 0.8 / 32)) * 32)
B_COLS = 3 * B_WIDTH + 2 * W_LORA + 2 * A_LORA + G_LORA
AB_COLS = A_COLS + B_COLS
D_FF = 256 * ((8 * D_MODEL // 3 + 255) // 256)
N_MOD = 9
RMS_EPS = 1e-6
LN_EPS = 1e-5
GN_EPS = 64e-5
W_DECAY_SCALE = math.exp(-0.5)

kernel_name = "hybrid_gmlp_rwkv7_shortconv_dit"


def rmsnorm(x, g):
    xf = x.astype(jnp.float32)
    y = xf * lax.rsqrt(jnp.mean(xf * xf, axis=-1, keepdims=True) + RMS_EPS)
    return (y * g).astype(x.dtype)


def modulate(h, shift, scale):
    return h * (1 + scale) + shift


def swiglu(h, w1, w3, w2):
    return (jax.nn.silu(h @ w1) * (h @ w3)) @ w2


def ffn_half(h_in, g, shift, scale, gate, w1, w3, w2):
    h = modulate(rmsnorm(h_in, g), shift, scale)
    return h_in + 0.5 * gate * swiglu(h, w1, w3, w2)


def centred_conv3(z, w, axis):
    n = z.shape[axis]
    pad = [(0, 0)] * z.ndim
    pad[axis] = (1, 1)
    zp = jnp.pad(z, pad)
    sl = lambda s: lax.slice_in_dim(zp, s, s + n, axis=axis)
    return w[0] * sl(0) + w[1] * sl(1) + w[2] * sl(2)


def token_shift(p, mu):
    n = p.shape[1]
    pp = jnp.pad(p, ((0, 0), (1, 1), (0, 0)))
    prev, nxt = pp[:, :n], pp[:, 2:]
    return p + mu[0] * (prev - p) + mu[1] * (nxt - p)


def chunk_gmlp(p, v_gain, ws, b):
    u = jax.nn.gelu(p[..., :A_WIDTH].astype(jnp.float32))
    v = jax.nn.gelu(p[..., A_WIDTH:].astype(jnp.float32))
    vm = jnp.mean(v, axis=-1, keepdims=True)
    vv = jnp.mean(jnp.square(v - vm), axis=-1, keepdims=True)
    v = (v - vm) * lax.rsqrt(vv + LN_EPS) * v_gain
    bn, n = p.shape[0], p.shape[1]
    v = v.reshape(bn, n // CHUNK, CHUNK, A_GROUPS, A_GROUP_DIM)
    vs = jnp.einsum('gts,bnsgc->bntgc', ws, v) + b.T[None, None, :, :, None]
    return u * vs.reshape(bn, n, A_WIDTH)


def rwkv7_prep(p, mu, w0, w2, a0, a2, g2, k_k, k_a):
    p = token_shift(p, mu).astype(jnp.float32)
    bn, n = p.shape[0], p.shape[1]
    heads = lambda t: t.reshape(bn, n, B_HEADS, HEAD_DIM)
    heads2 = lambda t: t.reshape(bn, n, 2, B_HEADS, HEAD_DIM)
    r = p[..., :B_WIDTH]
    k = p[..., B_WIDTH:2 * B_WIDTH]
    v = p[..., 2 * B_WIDTH:3 * B_WIDTH]
    o = 3 * B_WIDTH
    wd = p[..., o:o + 2 * W_LORA].reshape(bn, n, 2, W_LORA)
    o += 2 * W_LORA
    ad = p[..., o:o + 2 * A_LORA].reshape(bn, n, 2, A_LORA)
    o += 2 * A_LORA
    gd = p[..., o:o + G_LORA]
    w = jnp.exp(-W_DECAY_SCALE * jax.nn.sigmoid(w0 + jnp.einsum('bldr,drc->bldc', jnp.tanh(wd), w2)))
    a = jax.nn.sigmoid(a0 + jnp.einsum('bldr,drc->bldc', ad, a2))
    g = jax.nn.sigmoid(gd) @ g2
    kk = heads(k * k_k)
    kk = kk * lax.rsqrt(jnp.sum(kk * kk, axis=-1, keepdims=True) + 1e-12)
    kd = k[:, :, None] * (1 + (a - 1) * k_a)
    return heads(r), heads(v), kk, heads2(w), heads2(a), heads2(kd), g


def rwkv7_scan(s0, r, v, kk, w, a, k, reverse, emit):
    tm = lambda t: jnp.moveaxis(t, 1, 0)

    def step(s, inp):
        r_t, v_t, kk_t, w_t, a_t, k_t = inp
        s_kk = jnp.einsum('bhvk,bhk->bhv', s, kk_t)
        s = (s * w_t[:, :, None, :]
             - s_kk[..., None] * (kk_t * a_t)[:, :, None, :]
             + v_t[..., None] * k_t[:, :, None, :])
        return s, (jnp.einsum('bhvk,bhk->bhv', s, r_t) if emit else None)

    s_final, ys = lax.scan(step, s0, (tm(r), tm(v), tm(kk), tm(w), tm(a), tm(k)), reverse=reverse)
    return s_final, (jnp.moveaxis(ys, 0, 1) if emit else None)


def rwkv7_out(y, r, v, kd, g, r_k, ln_g, ln_b):
    bn, n = y.shape[0], y.shape[1]
    ym = jnp.mean(y, axis=-1, keepdims=True)
    yv = jnp.mean(jnp.square(y - ym), axis=-1, keepdims=True)
    yn = ((y - ym) * lax.rsqrt(yv + GN_EPS)).reshape(bn, n, B_WIDTH) * ln_g + ln_b
    kb = jnp.mean(kd, axis=2)
    bonus = (jnp.sum(r * kb * r_k, axis=-1, keepdims=True) * v).reshape(bn, n, B_WIDTH)
    return (yn + bonus) * g


def rwkv7_bidir(px, pc, ctx_out, mu, w0, w2, a0, a2, g2, k_k, k_a, r_k, ln_g, ln_b):
    fx = rwkv7_prep(px, mu, w0, w2, a0, a2, g2, k_k, k_a)
    fc = rwkv7_prep(pc, mu, w0, w2, a0, a2, g2, k_k, k_a)
    s0 = jnp.zeros((px.shape[0], B_HEADS, HEAD_DIM, HEAD_DIM), jnp.float32)
    y_x = 0.0
    y_c = 0.0
    for d, rev in enumerate((False, True)):
        s_ctx, yc = rwkv7_scan(s0, fc[0], fc[1], fc[2], fc[3][:, :, d], fc[4][:, :, d], fc[5][:, :, d], rev, ctx_out)
        _, yx = rwkv7_scan(s_ctx, fx[0], fx[1], fx[2], fx[3][:, :, d], fx[4][:, :, d], fx[5][:, :, d], rev, True)
        y_x = y_x + yx
        if ctx_out:
            y_c = y_c + yc
    out_x = rwkv7_out(y_x, fx[0], fx[1], fx[5], fx[6], r_k, ln_g, ln_b)
    out_c = rwkv7_out(y_c, fc[0], fc[1], fc[5], fc[6], r_k, ln_g, ln_b) if ctx_out else None
    return out_x, out_c


def hybrid_ab_mixer(hx, hc, ctx_out, w_in, w_out, v_gain, ws, b_s, mu, w0, w2, a0, a2, g2,
                    k_k, k_a, r_k, ln_g, ln_b):
    px = hx @ w_in
    pc = hc @ (w_in if ctx_out else w_in[:, A_COLS:])
    pc_b = pc[..., A_COLS:] if ctx_out else pc
    ya_x = chunk_gmlp(px[..., :A_COLS], v_gain, ws, b_s)
    yb_x, yb_c = rwkv7_bidir(px[..., A_COLS:], pc_b, ctx_out, mu, w0, w2, a0, a2, g2,
                             k_k, k_a, r_k, ln_g, ln_b)
    out_x = jnp.concatenate([ya_x.astype(hx.dtype), yb_x.astype(hx.dtype)], axis=-1) @ w_out
    out_c = None
    if ctx_out:
        ya_c = chunk_gmlp(pc[..., :A_COLS], v_gain, ws, b_s)
        out_c = jnp.concatenate([ya_c.astype(hc.dtype), yb_c.astype(hc.dtype)], axis=-1) @ w_out
    return out_x, out_c


def short_conv_mixer(h, w_in, conv_w, w_out, rows):
    p = h @ w_in
    gate_b, gate_c, hv = jnp.split(p, 3, axis=-1)
    z = gate_c * hv
    if rows is None:
        zc = centred_conv3(z, conv_w, axis=1)
    else:
        bn, n, ch = z.shape
        zc = centred_conv3(z.reshape(bn, rows, GRID_W, ch), conv_w, axis=2).reshape(bn, n, ch)
    return (gate_b * zc) @ w_out


def setup_inputs(seed: int = 0) -> dict:
    key = jax.random.key(seed)
    ks = iter(jax.random.split(key, 40))
    nrm = lambda shape, s: jax.random.normal(next(ks), shape, jnp.float32) * s
    unif = lambda shape, lo, hi: jax.random.uniform(next(ks), shape, jnp.float32, lo, hi)
    D = D_MODEL
    NE = (DEPTH + 1) // 2
    NO = DEPTH // 2
    return {
        "x": nrm((BATCH, SEQ, D), 1.0),
        "c": nrm((BATCH, D), 1.0),
        "ctx": nrm((BATCH, CTX_LEN, D), 1.0),
        "c_ctx": nrm((D,), 1.0),
        "ada_w": nrm((DEPTH, D, N_MOD * D), 0.25 * D ** -0.5),
        "ada_b": nrm((DEPTH, N_MOD * D), 0.02),
        "norm_g": 1.0 + nrm((DEPTH, 3, D), 0.02),
        "ffn_w1": nrm((DEPTH, 2, D, D_FF), D ** -0.5),
        "ffn_w3": nrm((DEPTH, 2, D, D_FF), D ** -0.5),
        "ffn_w2": nrm((DEPTH, 2, D_FF, D), D_FF ** -0.5),
        "ab_w_in": nrm((NE, D, AB_COLS), D ** -0.5),
        "ab_w_out": nrm((NE, D, D), D ** -0.5),
        "gm_v_gain": 1.0 + nrm((NE, A_WIDTH), 0.02),
        "gm_ws": nrm((NE, A_GROUPS, CHUNK, CHUNK), CHUNK ** -0.5),
        "gm_b": 1.0 + nrm((NE, A_GROUPS, CHUNK), 0.02),
        "rw_mu": unif((NE, 2, B_COLS), 0.0, 0.5),
        "rw_w0": nrm((NE, 2, B_WIDTH), 0.5),
        "rw_w2": nrm((NE, 2, W_LORA, B_WIDTH), W_LORA ** -0.5),
        "rw_a0": nrm((NE, 2, B_WIDTH), 0.5),
        "rw_a2": nrm((NE, 2, A_LORA, B_WIDTH), A_LORA ** -0.5),
        "rw_g2": nrm((NE, G_LORA, B_WIDTH), G_LORA ** -0.5),
        "rw_k_k": 0.85 + nrm((NE, B_WIDTH), 0.02),
        "rw_k_a": 1.0 + nrm((NE, B_WIDTH), 0.02),
        "rw_r_k": nrm((NE, B_HEADS, HEAD_DIM), 0.1),
        "rw_ln_g": 1.0 + nrm((NE, B_WIDTH), 0.02),
        "rw_ln_b": nrm((NE, B_WIDTH), 0.02),
        "sc_w_in": nrm((NO, D, 3 * D), D ** -0.5),
        "sc_conv": nrm((NO, 3, D), 3 ** -0.5),
        "sc_w_out": nrm((NO, D, D), D ** -0.5),
        "final_g": 1.0 + nrm((D,), 0.02),
    }


def reference(x, c, ctx, c_ctx, ada_w, ada_b, norm_g, ffn_w1, ffn_w3, ffn_w2,
              ab_w_in, ab_w_out, gm_v_gain, gm_ws, gm_b, rw_mu, rw_w0, rw_w2, rw_a0, rw_a2,
              rw_g2, rw_k_k, rw_k_a, rw_r_k, rw_ln_g, rw_ln_b, sc_w_in, sc_conv, sc_w_out,
              final_g):
    rows = x.shape[1] // GRID_W
    silu_c = jax.nn.silu(c)
    for i in range(DEPTH):
        even = i % 2 == 0
        ctx_after = any(j % 2 == 0 for j in range(i + 1, DEPTH))
        if not (even or ctx_after):
            ctx = None
        m = (silu_c @ ada_w[i] + ada_b[i]).reshape(c.shape[0], 1, N_MOD, D_MODEL)
        mx = [m[:, :, j] for j in range(N_MOD)]
        mc = None
        if ctx is not None:
            mcv = (jax.nn.silu(c_ctx) @ ada_w[i] + ada_b[i]).reshape(N_MOD, D_MODEL)
            mc = [mcv[j] for j in range(N_MOD)]

        x = ffn_half(x, norm_g[i, 0], mx[0], mx[1], mx[2], ffn_w1[i, 0], ffn_w3[i, 0], ffn_w2[i, 0])
        if ctx is not None:
            ctx = ffn_half(ctx, norm_g[i, 0], mc[0], mc[1], mc[2], ffn_w1[i, 0], ffn_w3[i, 0], ffn_w2[i, 0])

        hx = modulate(rmsnorm(x, norm_g[i, 1]), mx[3], mx[4])
        hc = None if ctx is None else modulate(rmsnorm(ctx, norm_g[i, 1]), mc[3], mc[4])
        if even:
            e = i // 2
            ox, oc = hybrid_ab_mixer(hx, hc, ctx_after, ab_w_in[e], ab_w_out[e], gm_v_gain[e], gm_ws[e],
                                     gm_b[e], rw_mu[e], rw_w0[e], rw_w2[e], rw_a0[e], rw_a2[e], rw_g2[e],
                                     rw_k_k[e], rw_k_a[e], rw_r_k[e], rw_ln_g[e], rw_ln_b[e])
        else:
            o = i // 2
            ox = short_conv_mixer(hx, sc_w_in[o], sc_conv[o], sc_w_out[o], rows)
            oc = short_conv_mixer(hc, sc_w_in[o], sc_conv[o], sc_w_out[o], None) if ctx_after else None
        x = x + mx[5] * ox

        x = ffn_half(x, norm_g[i, 2], mx[6], mx[7], mx[8], ffn_w1[i, 1], ffn_w3[i, 1], ffn_w2[i, 1])
        if ctx_after:
            ctx = ctx + mc[5] * oc
            ctx = ffn_half(ctx, norm_g[i, 2], mc[6], mc[7], mc[8], ffn_w1[i, 1], ffn_w3[i, 1], ffn_w2[i, 1])
        else:
            ctx = None
    return rmsnorm(x, final_g)
```

```python
import functools
import math

import jax
import jax.numpy as jnp
from jax import lax
from jax.experimental import pallas as pl
from jax.experimental.pallas import tpu as pltpu

F32 = jnp.float32
BF16 = jnp.bfloat16

GRID_W = 64
CHUNK = 128
HEAD_DIM = 64
RMS_EPS = 1e-6
LN_EPS = 1e-5
GN_EPS = 64e-5
KK_EPS = 1e-12
W_DECAY_SCALE = math.exp(-0.5)

LANE = 128
SUBLANE = 8
VMEM_LIMIT = 56 * 1024 * 1024


def _cparams(sem):
    return pltpu.CompilerParams(dimension_semantics=sem, vmem_limit_bytes=VMEM_LIMIT)


def _round_up(n, m):
    return (n + m - 1) // m * m


def _dot(a, b):
    return jnp.dot(a, b, preferred_element_type=F32)


def _dot2(a, b):
    hi = a.astype(BF16)
    lo = (a - hi.astype(F32)).astype(BF16)
    return _dot(hi, b) + _dot(lo, b)


def _prenorm(x, g, shift, scale):
    ms = jnp.mean(x * x, axis=-1, keepdims=True)
    return (x * lax.rsqrt(ms + RMS_EPS) * g) * (1.0 + scale) + shift


def _mod_body(c_ref, w_ref, b_ref, o_ref):
    s = jax.nn.silu(c_ref[...]).astype(BF16)
    o_ref[...] = _dot(s, w_ref[...].astype(BF16)) + b_ref[...]


def _modulation(c_rows, ada_w, ada_b):
    depth, d, n = ada_w.shape
    tn = 1024 if n % 1024 == 0 else 512
    rows = c_rows.shape[0]
    return pl.pallas_call(
        _mod_body,
        grid=(depth, n // tn),
        in_specs=[
            pl.BlockSpec((rows, d), lambda l, j: (0, 0)),
            pl.BlockSpec((None, d, tn), lambda l, j: (l, 0, j)),
            pl.BlockSpec((None, 1, tn), lambda l, j: (l, 0, j)),
        ],
        out_specs=pl.BlockSpec((None, rows, tn), lambda l, j: (l, 0, j)),
        out_shape=jax.ShapeDtypeStruct((depth, rows, n), F32),
        compiler_params=_cparams(("arbitrary", "arbitrary")),
    )(c_rows, ada_w, ada_b.reshape(depth, 1, n))


def _glu_body(*refs, n_up, kind, coef, final, nf):
    x_ref, g_ref, sh_ref, sc_ref, gt_ref = refs[:5]
    up_refs = refs[5:5 + n_up]
    down_ref = refs[5 + n_up]
    idx = 6 + n_up
    cw_ref = fg_ref = None
    if kind == "conv":
        cw_ref = refs[idx]
        idx += 1
    if final:
        fg_ref = refs[idx]
        idx += 1
    o_ref, h_ref = refs[idx], refs[idx + 1]
    f = pl.program_id(1)

    @pl.when(f == 0)
    def _():
        h_ref[...] = _prenorm(x_ref[...], g_ref[...], sh_ref[...], sc_ref[...]).astype(BF16)

    h = h_ref[...]
    ups = [_dot(h, u[...]) for u in up_refs]
    if kind == "swiglu":
        mid = jax.nn.silu(ups[0]) * ups[1]
    else:
        z = ups[1] * ups[2]
        rows = z.shape[0]
        col = lax.broadcasted_iota(jnp.int32, z.shape, 0) % GRID_W
        zp = jnp.where(col == 0, 0.0, pltpu.roll(z, 1, axis=0))
        zn = jnp.where(col == GRID_W - 1, 0.0, pltpu.roll(z, rows - 1, axis=0))
        cw = cw_ref[...]
        mid = ups[0] * (cw[0:1] * zp + cw[1:2] * z + cw[2:3] * zn)
    contrib = _dot(mid.astype(BF16), down_ref[...])

    @pl.when(f == 0)
    def _():
        o_ref[...] = contrib

    @pl.when(f > 0)
    def _():
        o_ref[...] += contrib

    @pl.when(f == nf - 1)
    def _():
        res = x_ref[...] + (coef * gt_ref[...]) * o_ref[...]
        if final:
            ms = jnp.mean(res * res, axis=-1, keepdims=True)
            res = res * lax.rsqrt(ms + RMS_EPS) * fg_ref[...]
        o_ref[...] = res


def _glu_block(x, g, shift, scale, gate, ups, down, *, kind, coef, conv_w=None, final_g=None,
               rows_per_mod, tm=512, tf=512):
    m, d = x.shape
    fdim = down.shape[0]
    tm = min(tm, m)
    nf = fdim // tf
    tpb = rows_per_mod // tm
    mod_spec = pl.BlockSpec((None, 1, d), lambda i, f: (i // tpb, 0, 0))
    in_specs = [
        pl.BlockSpec((tm, d), lambda i, f: (i, 0)),
        pl.BlockSpec((1, d), lambda i, f: (0, 0)),
        mod_spec, mod_spec, mod_spec,
    ]
    args = [x, g.reshape(1, d), shift, scale, gate]
    for u in ups:
        in_specs.append(pl.BlockSpec((d, tf), lambda i, f: (0, f)))
        args.append(u)
    in_specs.append(pl.BlockSpec((tf, d), lambda i, f: (f, 0)))
    args.append(down)
    if kind == "conv":
        in_specs.append(pl.BlockSpec((3, tf), lambda i, f: (0, f)))
        args.append(conv_w)
    if final_g is not None:
        in_specs.append(pl.BlockSpec((1, d), lambda i, f: (0, 0)))
        args.append(final_g.reshape(1, d))
    body = functools.partial(_glu_body, n_up=len(ups), kind=kind, coef=coef,
                             final=final_g is not None, nf=nf)
    return pl.pallas_call(
        body,
        grid=(m // tm, nf),
        in_specs=in_specs,
        out_specs=pl.BlockSpec((tm, d), lambda i, f: (i, 0)),
        out_shape=jax.ShapeDtypeStruct((m, d), F32),
        scratch_shapes=[pltpu.VMEM((tm, d), BF16)],
        compiler_params=_cparams(("arbitrary", "arbitrary")),
    )(*args)


def _proj_body(x_ref, g_ref, sh_ref, sc_ref, w_ref, o_ref, h_ref):
    @pl.when(pl.program_id(1) == 0)
    def _():
        h_ref[...] = _prenorm(x_ref[...], g_ref[...], sh_ref[...], sc_ref[...]).astype(BF16)

    o_ref[...] = _dot(h_ref[...], w_ref[...])


def _prenorm_proj(x, g, shift, scale, w, *, rows_per_mod, tm=512, tn=512):
    m, d = x.shape
    n = w.shape[1]
    tm = min(tm, m)
    tpb = rows_per_mod // tm
    mod_spec = pl.BlockSpec((None, 1, d), lambda i, j: (i // tpb, 0, 0))
    return pl.pallas_call(
        _proj_body,
        grid=(m // tm, n // tn),
        in_specs=[
            pl.BlockSpec((tm, d), lambda i, j: (i, 0)),
            pl.BlockSpec((1, d), lambda i, j: (0, 0)),
            mod_spec, mod_spec,
            pl.BlockSpec((d, tn), lambda i, j: (0, j)),
        ],
        out_specs=pl.BlockSpec((tm, tn), lambda i, j: (i, j)),
        out_shape=jax.ShapeDtypeStruct((m, n), F32),
        scratch_shapes=[pltpu.VMEM((tm, d), BF16)],
        compiler_params=_cparams(("arbitrary", "arbitrary")),
    )(x, g.reshape(1, d), shift, scale, w)


def _gmlp_body(p_ref, gain_ref, ws_ref, b_ref, o_ref, *, aw, groups, n_chunks):
    gd = aw // groups
    for c in range(n_chunks):
        rows = pl.ds(c * CHUNK, CHUNK)
        u = jax.nn.gelu(p_ref[rows, 0:aw])
        v = jax.nn.gelu(p_ref[rows, aw:2 * aw])
        vm = jnp.mean(v, axis=-1, keepdims=True)
        vc = v - vm
        vv = jnp.mean(vc * vc, axis=-1, keepdims=True)
        vn = (vc * lax.rsqrt(vv + LN_EPS) * gain_ref[...]).astype(BF16)
        for gi in range(groups):
            cols = slice(gi * gd, (gi + 1) * gd)
            vs = _dot(ws_ref[gi], vn[:, cols]) + b_ref[:, cols]
            o_ref[rows, cols] = (u[:, cols] * vs).astype(o_ref.dtype)


def _gmlp(pa, gain, ws, b_full, *, tm=512):
    m, two_aw = pa.shape
    aw = two_aw // 2
    groups = ws.shape[0]
    tm = min(tm, m)
    body = functools.partial(_gmlp_body, aw=aw, groups=groups, n_chunks=tm // CHUNK)
    return pl.pallas_call(
        body,
        grid=(m // tm,),
        in_specs=[
            pl.BlockSpec((tm, two_aw), lambda i: (i, 0)),
            pl.BlockSpec((1, aw), lambda i: (0, 0)),
            pl.BlockSpec((groups, CHUNK, CHUNK), lambda i: (0, 0, 0)),
            pl.BlockSpec((CHUNK, aw), lambda i: (0, 0)),
        ],
        out_specs=pl.BlockSpec((tm, aw), lambda i: (i, 0)),
        out_shape=jax.ShapeDtypeStruct((m, aw), BF16),
        compiler_params=_cparams(("arbitrary",)),
    )(pa, gain.reshape(1, aw), ws, b_full)


def _prep_body(p_ref, pp_ref, pn_ref, mu_ref, w0_ref, w2_ref, a0_ref, a2_ref, g2_ref, kk_ref, ka_ref,
               rk_ref, e_ref, r_o, v_o, kk_o, g_o, bonus_o, w_o, kka_o, kd_o, *, bw, wl2, al2, seq, tm):
    i = pl.program_id(0)
    p = p_ref[...]
    row = lax.broadcasted_iota(jnp.int32, p.shape, 0)
    first = (i * tm) % seq == 0
    last = ((i + 1) * tm) % seq == 0
    prev_row = jnp.where(first, 0.0, pp_ref[SUBLANE - 1:SUBLANE, :])
    next_row = jnp.where(last, 0.0, pn_ref[0:1, :])
    prev = jnp.where(row == 0, prev_row, pltpu.roll(p, 1, axis=0))
    nxt = jnp.where(row == tm - 1, next_row, pltpu.roll(p, tm - 1, axis=0))
    mu = mu_ref[...]
    p = p + mu[0:1] * (prev - p) + mu[1:2] * (nxt - p)

    r = p[:, 0:bw]
    k = p[:, bw:2 * bw]
    v = p[:, 2 * bw:3 * bw]
    o = 3 * bw
    wd = jnp.tanh(p[:, o:o + wl2]).astype(BF16)
    ad = p[:, o + wl2:o + wl2 + al2].astype(BF16)
    gdn = jax.nn.sigmoid(p[:, o + wl2 + al2:]).astype(BF16)

    w = jnp.exp(-W_DECAY_SCALE * jax.nn.sigmoid(w0_ref[...] + _dot(wd, w2_ref[...])))
    a = jax.nn.sigmoid(a0_ref[...] + _dot(ad, a2_ref[...]))
    g = _dot(gdn, g2_ref[...])

    e = e_ref[...]
    kq = k * kk_ref[...]
    kk = kq * lax.rsqrt(_dot2(kq * kq, e) + KK_EPS)
    ka = ka_ref[...]
    kd0 = k * (1.0 + (a[:, 0:bw] - 1.0) * ka)
    kd1 = k * (1.0 + (a[:, bw:] - 1.0) * ka)
    kb = 0.5 * (kd0 + kd1)
    bonus = _dot2(r * kb * rk_ref[...], e) * v

    r_o[...] = r
    v_o[...] = v
    kk_o[...] = kk
    g_o[...] = g
    bonus_o[...] = bonus
    w_o[...] = w
    kka_o[:, 0:bw] = kk * a[:, 0:bw]
    kka_o[:, bw:] = kk * a[:, bw:]
    kd_o[:, 0:bw] = kd0
    kd_o[:, bw:] = kd1


def _rwkv_prep(pb, seq, consts, *, tm=256):
    m, pw = pb.shape
    (mu, w0, w2cat, a0, a2cat, g2p, k_k, k_a, r_k, e) = consts
    bw = k_k.shape[1]
    wl2, al2 = w2cat.shape[0], a2cat.shape[0]
    tm = min(tm, seq)
    nsub = tm // SUBLANE
    last_blk = m // SUBLANE - 1
    full = lambda arr: pl.BlockSpec(arr.shape, lambda i: (0,) * arr.ndim)
    body = functools.partial(_prep_body, bw=bw, wl2=wl2, al2=al2, seq=seq, tm=tm)
    one = jax.ShapeDtypeStruct((m, bw), F32)
    two = jax.ShapeDtypeStruct((m, 2 * bw), F32)
    o1 = pl.BlockSpec((tm, bw), lambda i: (i, 0))
    o2 = pl.BlockSpec((tm, 2 * bw), lambda i: (i, 0))
    return pl.pallas_call(
        body,
        grid=(m // tm,),
        in_specs=[
            pl.BlockSpec((tm, pw), lambda i: (i, 0)),
            pl.BlockSpec((SUBLANE, pw), lambda i: (jnp.maximum(i * nsub - 1, 0), 0)),
            pl.BlockSpec((SUBLANE, pw), lambda i: (jnp.minimum((i + 1) * nsub, last_blk), 0)),
            full(mu), full(w0), full(w2cat), full(a0), full(a2cat), full(g2p), full(k_k), full(k_a),
            full(r_k), full(e),
        ],
        out_specs=[o1, o1, o1, o1, o1, o2, o2, o2],
        out_shape=[one, one, one, one, one, two, two, two],
        compiler_params=_cparams(("arbitrary",)),
    )(pb, pb, pb, mu, w0, w2cat, a0, a2cat, g2p, k_k, k_a, r_k, e)


def _scan_body(r_ref, v_ref, kk_ref, w_ref, kka_ref, kd_ref, y_ref, s_ref, *, tt, n):
    @pl.when(pl.program_id(0) == 0)
    def _():
        s_ref[...] = jnp.zeros_like(s_ref)

    def step(t, carry):
        r_t = r_ref[t]
        kk_t = kk_ref[t]
        w_t = w_ref[t]
        kka_t = kka_ref[t]
        kd_t = kd_ref[t]
        for vi in range(n):
            s = s_ref[vi]
            sk = jnp.sum(s * kk_t, axis=0, keepdims=True)
            s = s * w_t - sk * kka_t + v_ref[t, vi:vi + 1, :] * kd_t
            s_ref[vi] = s
            y_ref[t, vi:vi + 1, :] = jnp.sum(s * r_t, axis=0, keepdims=True)
        return carry

    lax.fori_loop(0, tt, step, 0)


def _rwkv_scan(r, v, kk, w, kka, kd, *, tt=32):
    steps, n, lanes = r.shape
    blk = pl.BlockSpec((tt, n, lanes), lambda i: (i, 0, 0))
    return pl.pallas_call(
        functools.partial(_scan_body, tt=tt, n=n),
        grid=(steps // tt,),
        in_specs=[blk] * 6,
        out_specs=blk,
        out_shape=jax.ShapeDtypeStruct((steps, n, lanes), F32),
        scratch_shapes=[pltpu.VMEM((n, n, lanes), F32)],
        compiler_params=_cparams(("arbitrary",)),
    )(r, v, kk, w, kka, kd)


def _mixout_body(x_ref, gt_ref, ya_ref, y_ref, bonus_ref, g_ref, lng_ref, lnb_ref, e_ref, wa_ref, wb_ref,
                 o_ref):
    e = e_ref[...]
    inv_n = 1.0 / HEAD_DIM
    y = y_ref[...]
    yc = y - _dot2(y, e) * inv_n
    var = _dot2(yc * yc, e) * inv_n
    yn = yc * lax.rsqrt(var + GN_EPS) * lng_ref[...] + lnb_ref[...]
    yb = ((yn + bonus_ref[...]) * g_ref[...]).astype(BF16)
    ox = _dot(ya_ref[...], wa_ref[...]) + _dot(yb, wb_ref[...])
    o_ref[...] = x_ref[...] + gt_ref[...] * ox


def _mix_out(x, gate, ya, y, bonus, g, ln_g, ln_b, e, w_out_a, w_out_b, *, rows_per_mod, tm=256):
    m, d = x.shape
    aw, bw = ya.shape[1], y.shape[1]
    tpb = rows_per_mod // tm
    full = lambda arr: pl.BlockSpec(arr.shape, lambda i: (0,) * arr.ndim)
    tok = lambda width: pl.BlockSpec((tm, width), lambda i: (i, 0))
    return pl.pallas_call(
        _mixout_body,
        grid=(m // tm,),
        in_specs=[
            tok(d), pl.BlockSpec((None, 1, d), lambda i: (i // tpb, 0, 0)),
            tok(aw), tok(bw), tok(bw), tok(bw),
            full(ln_g), full(ln_b), full(e), full(w_out_a), full(w_out_b),
        ],
        out_specs=tok(d),
        out_shape=jax.ShapeDtypeStruct((m, d), F32),
        compiler_params=_cparams(("arbitrary",)),
    )(x, gate, ya, y, bonus, g, ln_g, ln_b, e, w_out_a, w_out_b)


def _to_scan_layout(q_ctx, q_x, bn, heads, two_dirs):
    def seqs(q, d):
        q = q.reshape(bn, -1, q.shape[-1])
        if two_dirs:
            half = q.shape[-1] // 2
            q = q[..., d * half:(d + 1) * half]
        return q[:, ::-1] if d == 1 else q

    per_dir = [jnp.concatenate([seqs(q_ctx, d), seqs(q_x, d)], axis=1) for d in (0, 1)]
    q = jnp.stack(per_dir, axis=0)
    steps = q.shape[2]
    q = q.reshape(2, bn, steps, heads, HEAD_DIM)
    return jnp.transpose(q, (2, 4, 0, 1, 3)).reshape(steps, HEAD_DIM, 2 * bn * heads)


def kernel(x, c, ctx, c_ctx, ada_w, ada_b, norm_g, ffn_w1, ffn_w3, ffn_w2, ab_w_in, ab_w_out, gm_v_gain,
           gm_ws, gm_b, rw_mu, rw_w0, rw_w2, rw_a0, rw_a2, rw_g2, rw_k_k, rw_k_a, rw_r_k, rw_ln_g, rw_ln_b,
           sc_w_in, sc_conv, sc_w_out, final_g):
    bn, seq, d = x.shape
    ctx_len = ctx.shape[1]
    depth = ada_w.shape[0]
    n_mod = ada_w.shape[2] // d
    m_x, m_c = bn * seq, bn * ctx_len
    xs = x.reshape(m_x, d)
    cs = ctx.reshape(m_c, d)

    rows = _round_up(bn + 1, SUBLANE)
    c_rows = jnp.zeros((rows, d), F32).at[:bn].set(c).at[bn].set(c_ctx)
    mods = _modulation(c_rows, ada_w, ada_b).reshape(depth, rows, n_mod, d)

    for i in range(depth):
        even = i % 2 == 0
        ctx_after = any(j % 2 == 0 for j in range(i + 1, depth))
        use_ctx = cs is not None and (even or ctx_after)
        if not use_ctx:
            cs = None
        mx = [mods[i, :bn, j].reshape(bn, 1, d) for j in range(n_mod)]
        mc = [mods[i, bn:bn + 1, j].reshape(1, 1, d) for j in range(n_mod)]
        w1 = ffn_w1[i].astype(BF16)
        w3 = ffn_w3[i].astype(BF16)
        w2 = ffn_w2[i].astype(BF16)
        last_layer = i == depth - 1

        xs = _glu_block(xs, norm_g[i, 0], mx[0], mx[1], mx[2], [w1[0], w3[0]], w2[0], kind="swiglu",
                        coef=0.5, rows_per_mod=seq)
        if cs is not None:
            cs = _glu_block(cs, norm_g[i, 0], mc[0], mc[1], mc[2], [w1[0], w3[0]], w2[0], kind="swiglu",
                            coef=0.5, rows_per_mod=m_c)

        if even:
            e_idx = i // 2
            if ctx_after:
                raise NotImplementedError("context output of an even layer is only needed for depth > 2")
            xs = _ab_mixer(xs, cs, mx, mc, norm_g[i, 1], bn, seq, ctx_len, ab_w_in[e_idx], ab_w_out[e_idx],
                           gm_v_gain[e_idx], gm_ws[e_idx], gm_b[e_idx], rw_mu[e_idx], rw_w0[e_idx],
                           rw_w2[e_idx], rw_a0[e_idx], rw_a2[e_idx], rw_g2[e_idx], rw_k_k[e_idx],
                           rw_k_a[e_idx], rw_r_k[e_idx], rw_ln_g[e_idx], rw_ln_b[e_idx])
        else:
            o_idx = i // 2
            if ctx_after:
                raise NotImplementedError("context output of an odd layer is only needed for depth > 2")
            w_in = sc_w_in[o_idx].astype(BF16)
            ups = [w_in[:, 0:d], w_in[:, d:2 * d], w_in[:, 2 * d:3 * d]]
            xs = _glu_block(xs, norm_g[i, 1], mx[3], mx[4], mx[5], ups, sc_w_out[o_idx].astype(BF16),
                            kind="conv", coef=1.0, conv_w=sc_conv[o_idx], rows_per_mod=seq)
        cs = None

        xs = _glu_block(xs, norm_g[i, 2], mx[6], mx[7], mx[8], [w1[1], w3[1]], w2[1], kind="swiglu",
                        coef=0.5, final_g=final_g if last_layer else None, rows_per_mod=seq)
    return xs.reshape(bn, seq, d)


def _ab_mixer(xs, cs, mx, mc, g, bn, seq, ctx_len, w_in, w_out, v_gain, ws, b_s, mu, w0, w2, a0, a2, g2,
              k_k, k_a, r_k, ln_g, ln_b):
    m_x, d = xs.shape
    m_c = cs.shape[0]
    bw = k_k.shape[0]
    aw = v_gain.shape[0]
    a_cols = 2 * aw
    heads = bw // HEAD_DIM
    wl, al, gl = w2.shape[1], a2.shape[1], g2.shape[0]
    b_cols = w_in.shape[1] - a_cols
    pw = _round_up(b_cols, 512)
    gp = pw - (3 * bw + 2 * wl + 2 * al)

    w_in_bf = w_in.astype(BF16)
    w_in_a = w_in_bf[:, :a_cols]
    w_in_b = jnp.pad(w_in_bf[:, a_cols:], ((0, 0), (0, pw - b_cols)))

    mu_p = jnp.pad(mu, ((0, 0), (0, pw - b_cols)))
    zeros_w = jnp.zeros((wl, bw), F32)
    zeros_a = jnp.zeros((al, bw), F32)
    w2cat = jnp.concatenate([jnp.concatenate([w2[0], zeros_w], 1), jnp.concatenate([zeros_w, w2[1]], 1)], 0)
    a2cat = jnp.concatenate([jnp.concatenate([a2[0], zeros_a], 1), jnp.concatenate([zeros_a, a2[1]], 1)], 0)
    g2p = jnp.pad(g2, ((0, gp - gl), (0, 0)))
    head_id = jnp.arange(bw) // HEAD_DIM
    e = (head_id[:, None] == head_id[None, :]).astype(BF16)
    consts = (mu_p, w0.reshape(1, 2 * bw), w2cat.astype(BF16), a0.reshape(1, 2 * bw), a2cat.astype(BF16),
              g2p.astype(BF16), k_k.reshape(1, bw), k_a.reshape(1, bw), r_k.reshape(1, bw), e)

    pa_x = _prenorm_proj(xs, g, mx[3], mx[4], w_in_a, rows_per_mod=seq)
    pb_x = _prenorm_proj(xs, g, mx[3], mx[4], w_in_b, rows_per_mod=seq)
    pb_c = _prenorm_proj(cs, g, mc[3], mc[4], w_in_b, rows_per_mod=m_c)

    b_full = jnp.repeat(b_s.T, aw // ws.shape[0], axis=1)
    ya = _gmlp(pa_x, v_gain, ws.astype(BF16), b_full)

    fx = _rwkv_prep(pb_x, seq, consts)
    fc = _rwkv_prep(pb_c, ctx_len, consts)
    r_x, v_x, kk_x, g_x, bonus_x, w_x, kka_x, kd_x = fx
    r_c, v_c, kk_c, _, _, w_c, kka_c, kd_c = fc
    lay1 = lambda qc, qx: _to_scan_layout(qc, qx, bn, heads, False)
    lay2 = lambda qc, qx: _to_scan_layout(qc, qx, bn, heads, True)
    y_scan = _rwkv_scan(lay1(r_c, r_x), lay1(v_c, v_x), lay1(kk_c, kk_x), lay2(w_c, w_x), lay2(kka_c, kka_x),
                        lay2(kd_c, kd_x))
    y_scan = y_scan[ctx_len:].reshape(seq, HEAD_DIM, 2, bn, heads)
    y_x = y_scan[:, :, 0] + y_scan[::-1, :, 1]
    y_x = jnp.transpose(y_x, (2, 0, 3, 1)).reshape(m_x, bw)

    w_out_bf = w_out.astype(BF16)
    return _mix_out(xs, mx[5], ya, y_x, bonus_x, g_x, ln_g.reshape(1, bw), ln_b.reshape(1, bw), e,
                    w_out_bf[:aw], w_out_bf[aw:], rows_per_mod=seq)
```

```python
import functools
import math

import jax
import jax.numpy as jnp
from jax import lax
from jax.experimental import pallas as pl
from jax.experimental.pallas import tpu as pltpu

F32 = jnp.float32
BF16 = jnp.bfloat16

GRID_W = 64
CHUNK = 128
HEAD_DIM = 64
RMS_EPS = 1e-6
LN_EPS = 1e-5
GN_EPS = 64e-5
KK_EPS = 1e-12
W_DECAY_SCALE = math.exp(-0.5)

LANE = 128
SUBLANE = 8
VMEM_LIMIT = 56 * 1024 * 1024


def _cparams(sem):
    return pltpu.CompilerParams(dimension_semantics=sem, vmem_limit_bytes=VMEM_LIMIT)


def _round_up(n, m):
    return (n + m - 1) // m * m


def _dot(a, b):
    return jnp.dot(a, b, preferred_element_type=F32)


def _dot2(a, b):
    hi = a.astype(BF16)
    lo = (a - hi.astype(F32)).astype(BF16)
    return _dot(hi, b) + _dot(lo, b)


def _prenorm(x, g, shift, scale):
    ms = jnp.mean(x * x, axis=-1, keepdims=True)
    return (x * lax.rsqrt(ms + RMS_EPS) * g) * (1.0 + scale) + shift


def _mod_body(c_ref, w_ref, b_ref, o_ref):
    s = jax.nn.silu(c_ref[...]).astype(BF16)
    o_ref[...] = _dot(s, w_ref[...].astype(BF16)) + b_ref[...]


def _modulation(c_rows, ada_w, ada_b):
    depth, d, n = ada_w.shape
    tn = 1024 if n % 1024 == 0 else 512
    rows = c_rows.shape[0]
    return pl.pallas_call(
        _mod_body,
        grid=(depth, n // tn),
        in_specs=[
            pl.BlockSpec((rows, d), lambda l, j: (0, 0)),
            pl.BlockSpec((None, d, tn), lambda l, j: (l, 0, j)),
            pl.BlockSpec((None, 1, tn), lambda l, j: (l, 0, j)),
        ],
        out_specs=pl.BlockSpec((None, rows, tn), lambda l, j: (l, 0, j)),
        out_shape=jax.ShapeDtypeStruct((depth, rows, n), F32),
        compiler_params=_cparams(("arbitrary", "arbitrary")),
    )(c_rows, ada_w, ada_b.reshape(depth, 1, n))


def _glu_body(*refs, n_up, kind, coef, final, nf):
    x_ref, g_ref, sh_ref, sc_ref, gt_ref = refs[:5]
    up_refs = refs[5:5 + n_up]
    down_ref = refs[5 + n_up]
    idx = 6 + n_up
    cw_ref = fg_ref = None
    if kind == "conv":
        cw_ref = refs[idx]
        idx += 1
    if final:
        fg_ref = refs[idx]
        idx += 1
    o_ref, h_ref = refs[idx], refs[idx + 1]
    f = pl.program_id(1)

    @pl.when(f == 0)
    def _():
        h_ref[...] = _prenorm(x_ref[...], g_ref[...], sh_ref[...], sc_ref[...]).astype(BF16)

    h = h_ref[...]
    ups = [_dot(h, u[...]) for u in up_refs]
    if kind == "swiglu":
        mid = jax.nn.silu(ups[0]) * ups[1]
    else:
        z = ups[1] * ups[2]
        rows = z.shape[0]
        col = lax.broadcasted_iota(jnp.int32, z.shape, 0) % GRID_W
        zp = jnp.where(col == 0, 0.0, pltpu.roll(z, 1, axis=0))
        zn = jnp.where(col == GRID_W - 1, 0.0, pltpu.roll(z, rows - 1, axis=0))
        cw = cw_ref[...]
        mid = ups[0] * (cw[0:1] * zp + cw[1:2] * z + cw[2:3] * zn)
    contrib = _dot(mid.astype(BF16), down_ref[...])

    @pl.when(f == 0)
    def _():
        o_ref[...] = contrib

    @pl.when(f > 0)
    def _():
        o_ref[...] += contrib

    @pl.when(f == nf - 1)
    def _():
        res = x_ref[...] + (coef * gt_ref[...]) * o_ref[...]
        if final:
            ms = jnp.mean(res * res, axis=-1, keepdims=True)
            res = res * lax.rsqrt(ms + RMS_EPS) * fg_ref[...]
        o_ref[...] = res


def _glu_block(x, g, shift, scale, gate, ups, down, *, kind, coef, conv_w=None, final_g=None,
               rows_per_mod, tm=512, tf=512):
    m, d = x.shape
    fdim = down.shape[0]
    tm = min(tm, m)
    nf = fdim // tf
    tpb = rows_per_mod // tm
    mod_spec = pl.BlockSpec((None, 1, d), lambda i, f: (i // tpb, 0, 0))
    in_specs = [
        pl.BlockSpec((tm, d), lambda i, f: (i, 0)),
        pl.BlockSpec((1, d), lambda i, f: (0, 0)),
        mod_spec, mod_spec, mod_spec,
    ]
    args = [x, g.reshape(1, d), shift, scale, gate]
    for u in ups:
        in_specs.append(pl.BlockSpec((d, tf), lambda i, f: (0, f)))
        args.append(u)
    in_specs.append(pl.BlockSpec((tf, d), lambda i, f: (f, 0)))
    args.append(down)
    if kind == "conv":
        in_specs.append(pl.BlockSpec((3, tf), lambda i, f: (0, f)))
        args.append(conv_w)
    if final_g is not None:
        in_specs.append(pl.BlockSpec((1, d), lambda i, f: (0, 0)))
        args.append(final_g.reshape(1, d))
    body = functools.partial(_glu_body, n_up=len(ups), kind=kind, coef=coef,
                             final=final_g is not None, nf=nf)
    return pl.pallas_call(
        body,
        grid=(m // tm, nf),
        in_specs=in_specs,
        out_specs=pl.BlockSpec((tm, d), lambda i, f: (i, 0)),
        out_shape=jax.ShapeDtypeStruct((m, d), F32),
        scratch_shapes=[pltpu.VMEM((tm, d), BF16)],
        compiler_params=_cparams(("arbitrary", "arbitrary")),
    )(*args)


def _proj_body(x_ref, g_ref, sh_ref, sc_ref, w_ref, o_ref, h_ref):
    @pl.when(pl.program_id(1) == 0)
    def _():
        h_ref[...] = _prenorm(x_ref[...], g_ref[...], sh_ref[...], sc_ref[...]).astype(BF16)

    o_ref[...] = _dot(h_ref[...], w_ref[...])


def _prenorm_proj(x, g, shift, scale, w, *, rows_per_mod, tm=512, tn=512):
    m, d = x.shape
    n = w.shape[1]
    tm = min(tm, m)
    tpb = rows_per_mod // tm
    mod_spec = pl.BlockSpec((None, 1, d), lambda i, j: (i // tpb, 0, 0))
    return pl.pallas_call(
        _proj_body,
        grid=(m // tm, n // tn),
        in_specs=[
            pl.BlockSpec((tm, d), lambda i, j: (i, 0)),
            pl.BlockSpec((1, d), lambda i, j: (0, 0)),
            mod_spec, mod_spec,
            pl.BlockSpec((d, tn), lambda i, j: (0, j)),
        ],
        out_specs=pl.BlockSpec((tm, tn), lambda i, j: (i, j)),
        out_shape=jax.ShapeDtypeStruct((m, n), F32),
        scratch_shapes=[pltpu.VMEM((tm, d), BF16)],
        compiler_params=_cparams(("arbitrary", "arbitrary")),
    )(x, g.reshape(1, d), shift, scale, w)


def _gmlp_body(p_ref, gain_ref, ws_ref, b_ref, o_ref, *, aw, groups, n_chunks):
    gd = aw // groups
    for c in range(n_chunks):
        rows = pl.ds(c * CHUNK, CHUNK)
        u = jax.nn.gelu(p_ref[rows, 0:aw])
        v = jax.nn.gelu(p_ref[rows, aw:2 * aw])
        vm = jnp.mean(v, axis=-1, keepdims=True)
        vc = v - vm
        vv = jnp.mean(vc * vc, axis=-1, keepdims=True)
        vn = (vc * lax.rsqrt(vv + LN_EPS) * gain_ref[...]).astype(BF16)
        for gi in range(groups):
            cols = slice(gi * gd, (gi + 1) * gd)
            vs = _dot(ws_ref[gi], vn[:, cols]) + b_ref[:, cols]
            o_ref[rows, cols] = (u[:, cols] * vs).astype(o_ref.dtype)


def _gmlp(pa, gain, ws, b_full, *, tm=512):
    m, two_aw = pa.shape
    aw = two_aw // 2
    groups = ws.shape[0]
    tm = min(tm, m)
    body = functools.partial(_gmlp_body, aw=aw, groups=groups, n_chunks=tm // CHUNK)
    return pl.pallas_call(
        body,
        grid=(m // tm,),
        in_specs=[
            pl.BlockSpec((tm, two_aw), lambda i: (i, 0)),
            pl.BlockSpec((1, aw), lambda i: (0, 0)),
            pl.BlockSpec((groups, CHUNK, CHUNK), lambda i: (0, 0, 0)),
            pl.BlockSpec((CHUNK, aw), lambda i: (0, 0)),
        ],
        out_specs=pl.BlockSpec((tm, aw), lambda i: (i, 0)),
        out_shape=jax.ShapeDtypeStruct((m, aw), BF16),
        compiler_params=_cparams(("arbitrary",)),
    )(pa, gain.reshape(1, aw), ws, b_full)


def _prep_body(p_ref, pp_ref, pn_ref, mu_ref, w0_ref, w2_ref, a0_ref, a2_ref, g2_ref, kk_ref, ka_ref,
               rk_ref, e_ref, r_o, v_o, kk_o, g_o, bonus_o, lw_o, kka_o, kd_o, *, bw, wl2, al2, seq, tm):
    i = pl.program_id(0)
    p = p_ref[...]
    row = lax.broadcasted_iota(jnp.int32, p.shape, 0)
    first = (i * tm) % seq == 0
    last = ((i + 1) * tm) % seq == 0
    prev_row = jnp.where(first, 0.0, pp_ref[SUBLANE - 1:SUBLANE, :])
    next_row = jnp.where(last, 0.0, pn_ref[0:1, :])
    prev = jnp.where(row == 0, prev_row, pltpu.roll(p, 1, axis=0))
    nxt = jnp.where(row == tm - 1, next_row, pltpu.roll(p, tm - 1, axis=0))
    mu = mu_ref[...]
    p = p + mu[0:1] * (prev - p) + mu[1:2] * (nxt - p)

    r = p[:, 0:bw]
    k = p[:, bw:2 * bw]
    v = p[:, 2 * bw:3 * bw]
    o = 3 * bw
    wd = jnp.tanh(p[:, o:o + wl2]).astype(BF16)
    ad = p[:, o + wl2:o + wl2 + al2].astype(BF16)
    gdn = jax.nn.sigmoid(p[:, o + wl2 + al2:]).astype(BF16)

    lw = -W_DECAY_SCALE * jax.nn.sigmoid(w0_ref[...] + _dot(wd, w2_ref[...]))
    a = jax.nn.sigmoid(a0_ref[...] + _dot(ad, a2_ref[...]))
    g = _dot(gdn, g2_ref[...])

    e = e_ref[...]
    kq = k * kk_ref[...]
    kk = kq * lax.rsqrt(_dot2(kq * kq, e) + KK_EPS)
    ka = ka_ref[...]
    kd0 = k * (1.0 + (a[:, 0:bw] - 1.0) * ka)
    kd1 = k * (1.0 + (a[:, bw:] - 1.0) * ka)
    kb = 0.5 * (kd0 + kd1)
    bonus = _dot2(r * kb * rk_ref[...], e) * v

    r_o[...] = r
    v_o[...] = v
    kk_o[...] = kk
    g_o[...] = g
    bonus_o[...] = bonus
    lw_o[...] = lw
    kka_o[:, 0:bw] = kk * a[:, 0:bw]
    kka_o[:, bw:] = kk * a[:, bw:]
    kd_o[:, 0:bw] = kd0
    kd_o[:, bw:] = kd1


def _rwkv_prep(pb, seq, consts, *, tm=256):
    m, pw = pb.shape
    (mu, w0, w2cat, a0, a2cat, g2p, k_k, k_a, r_k, e) = consts
    bw = k_k.shape[1]
    wl2, al2 = w2cat.shape[0], a2cat.shape[0]
    tm = min(tm, seq)
    nsub = tm // SUBLANE
    last_blk = m // SUBLANE - 1
    full = lambda arr: pl.BlockSpec(arr.shape, lambda i: (0,) * arr.ndim)
    body = functools.partial(_prep_body, bw=bw, wl2=wl2, al2=al2, seq=seq, tm=tm)
    one = jax.ShapeDtypeStruct((m, bw), F32)
    two = jax.ShapeDtypeStruct((m, 2 * bw), F32)
    o1 = pl.BlockSpec((tm, bw), lambda i: (i, 0))
    o2 = pl.BlockSpec((tm, 2 * bw), lambda i: (i, 0))
    return pl.pallas_call(
        body,
        grid=(m // tm,),
        in_specs=[
            pl.BlockSpec((tm, pw), lambda i: (i, 0)),
            pl.BlockSpec((SUBLANE, pw), lambda i: (jnp.maximum(i * nsub - 1, 0), 0)),
            pl.BlockSpec((SUBLANE, pw), lambda i: (jnp.minimum((i + 1) * nsub, last_blk), 0)),
            full(mu), full(w0), full(w2cat), full(a0), full(a2cat), full(g2p), full(k_k), full(k_a),
            full(r_k), full(e),
        ],
        out_specs=[o1, o1, o1, o1, o1, o2, o2, o2],
        out_shape=[one, one, one, one, one, two, two, two],
        compiler_params=_cparams(("arbitrary",)),
    )(pb, pb, pb, mu, w0, w2cat, a0, a2cat, g2p, k_k, k_a, r_k, e)


SCAN_CHUNK = 64
INV_BLOCK = 16


def _dot_nt(a, b):
    return lax.dot_general(a, b, (((1,), (1,)), ((), ())), preferred_element_type=F32)


def _dot_tn(a, b):
    return lax.dot_general(a, b, (((0,), (0,)), ((), ())), preferred_element_type=F32)


def _pair_blockdiag(x, lane_lo):
    return jnp.concatenate([jnp.where(lane_lo, x, 0.0), jnp.where(lane_lo, 0.0, x)], axis=0).astype(BF16)


def _chunk_scan_body(r_ref, v_ref, kk_ref, lw_ref, kka_ref, kd_ref, s0_ref, y_ref, sout_ref,
                     s_ref, qw_ref, u0_ref, arb_ref, pe_ref, vb_ref, elt_ref, *, reverse, nsub, pairs):
    c = SCAN_CHUNK
    step = pl.program_id(1)
    bf = lambda t: t.astype(BF16)

    @pl.when(step == 0)
    def _():
        s_ref[...] = s0_ref[...]

    ri = lax.broadcasted_iota(jnp.int32, (c, c), 0)
    ci = lax.broadcasted_iota(jnp.int32, (c, c), 1)
    tri = ((ri <= ci) if reverse else (ri >= ci)).astype(BF16)
    rp = lax.broadcasted_iota(jnp.int32, (c, LANE), 0)
    cp = lax.broadcasted_iota(jnp.int32, (c, LANE), 1) % c
    incl_p = (rp <= cp) if reverse else (rp >= cp)
    strict_p = (rp < cp) if reverse else (rp > cp)
    blk_p = (rp // INV_BLOCK) == (cp // INV_BLOCK)
    eye_p = (rp == cp).astype(F32)
    lane_lo = lax.broadcasted_iota(jnp.int32, (c, LANE), 1) < HEAD_DIM
    bi = lax.broadcasted_iota(jnp.int32, (LANE, LANE), 0) // HEAD_DIM
    bj = lax.broadcasted_iota(jnp.int32, (LANE, LANE), 1) // HEAD_DIM
    same_head = bi == bj
    bd = lambda t: _pair_blockdiag(t, lane_lo)
    prange = range(pairs)
    lanes = [pl.ds(p * LANE, LANE) for p in prange]

    def phase_a(j, carry):
        rows = pl.ds(pl.multiple_of(j * c, c), c)
        lw = [lw_ref[rows, lanes[p]] for p in prange]
        lc = []
        for p in prange:
            hi = bf(lw[p])
            lc.append(_dot(tri, hi) + _dot(tri, bf(lw[p] - hi.astype(F32))))
        qq, qk_bd, qr, pk_bd, pb_bd, v_bd = [], [], [], [], [], []
        for p in prange:
            ltot = lc[p][0:1] if reverse else lc[p][c - 1:c]
            e_neg = jnp.exp(-lc[p])
            e_end = jnp.exp(ltot - lc[p])
            kd = kd_ref[rows, lanes[p]]
            kka = kka_ref[rows, lanes[p]]
            v = v_ref[rows, lanes[p]]
            qk = kk_ref[rows, lanes[p]] * jnp.exp(lc[p] - lw[p])
            qr.append(r_ref[rows, lanes[p]] * jnp.exp(lc[p]))
            qq.append(bf(jnp.concatenate([qk, qr[p]], axis=0)))
            qk_bd.append(bd(qk))
            pk_bd.append(bd(kd * e_neg))
            pb_bd.append(bd(kka * e_neg))
            v_bd.append(bd(v))
            pe_ref[j, p] = bf(jnp.concatenate([kd * e_end, -(kka * e_end)], axis=0))
            vb_ref[j, p] = bf(v)
            elt_ref[j, p] = jnp.broadcast_to(jnp.exp(ltot), (SUBLANE, LANE))
        g1 = [_dot_nt(qq[p], pk_bd[p]) for p in prange]
        g2 = [_dot_nt(qq[p], pb_bd[p]) for p in prange]
        akk = [bf(jnp.where(strict_p, g1[p][:c], 0.0)) for p in prange]
        ark = [bf(jnp.where(incl_p, g1[p][c:], 0.0)) for p in prange]
        nmat = [jnp.where(strict_p, g2[p][:c], 0.0) for p in prange]
        for p in prange:
            arb_ref[j, p] = bf(jnp.where(incl_p, g2[p][c:], 0.0))
        av = [_dot(akk[p], v_bd[p]) for p in prange]
        for p in prange:
            y_ref[rows, lanes[p]] = _dot(ark[p], v_bd[p])
        nd = [jnp.where(blk_p, nmat[p], 0.0) for p in prange]
        lo_bd = [bd(nmat[p] - nd[p]) for p in prange]
        mpow = [-nd[p] for p in prange]
        dinv = [eye_p + mpow[p] for p in prange]
        for _ in range(int(math.log2(INV_BLOCK)) - 1):
            mpow = [_dot(bf(mpow[p]), bd(mpow[p])) for p in prange]
            dinv = [dinv[p] + _dot(bf(dinv[p]), bd(mpow[p])) for p in prange]
        dinv_b = [bf(dinv[p]) for p in prange]
        x1 = [_dot(dinv_b[p], lo_bd[p]) for p in prange]
        x1b = [bf(x1[p]) for p in prange]
        acc = [eye_p - x1[p] for p in prange]
        xp = x1
        for k in range(2, SCAN_CHUNK // INV_BLOCK):
            xp = [_dot(x1b[p], bd(xp[p])) for p in prange]
            acc = [acc[p] + xp[p] if k % 2 == 0 else acc[p] - xp[p] for p in prange]
        tinv = [bf(_dot(bf(acc[p]), bd(dinv[p]))) for p in prange]
        wt = [_dot(tinv[p], qk_bd[p]) for p in prange]
        for p in prange:
            u0_ref[j, p] = _dot(tinv[p], bd(av[p]))
            qw_ref[j, p] = bf(jnp.concatenate([wt[p], qr[p]], axis=0))
        return carry

    lax.fori_loop(0, nsub, phase_a, 0)

    def phase_b(j, carry):
        jj = (nsub - 1 - j) if reverse else j
        rows = pl.ds(pl.multiple_of(jj * c, c), c)
        s = [s_ref[p] for p in prange]
        xs = [_dot_nt(qw_ref[jj, p], bf(s[p])) for p in prange]
        u = [u0_ref[jj, p] + xs[p][:c] for p in prange]
        au = [_dot(arb_ref[jj, p], bd(u[p])) for p in prange]
        for p in prange:
            y_ref[rows, lanes[p]] = y_ref[rows, lanes[p]] + xs[p][c:] - au[p]
        upd = [_dot_tn(jnp.concatenate([vb_ref[jj, p], bf(u[p])], axis=0), pe_ref[jj, p]) for p in prange]
        for p in prange:
            s_ref[p] = s[p] * elt_ref[jj, p][0:1] + jnp.where(same_head, upd[p], 0.0)
        return carry

    lax.fori_loop(0, nsub, phase_b, 0)

    @pl.when(step == pl.num_programs(1) - 1)
    def _():
        sout_ref[...] = s_ref[...]


def _chunk_scan(r, v, kk, lw, kka, kd, s0, *, bn, reverse, tt=256):
    m, bw = r.shape
    seq = m // bn
    tt = min(tt, seq)
    steps = seq // tt
    pairs = bw // LANE
    d = 1 if reverse else 0

    def row_blk(b, s):
        return b * steps + (steps - 1 - s if reverse else s)

    tok = pl.BlockSpec((tt, bw), lambda b, s: (row_blk(b, s), 0))
    tok_dir = pl.BlockSpec((tt, bw), lambda b, s: (row_blk(b, s), d))
    st = pl.BlockSpec((None, pairs, LANE, LANE), lambda b, s: (b, 0, 0, 0))
    c = SCAN_CHUNK
    nsub = tt // c
    body = functools.partial(_chunk_scan_body, reverse=reverse, nsub=nsub, pairs=pairs)
    return pl.pallas_call(
        body,
        grid=(bn, steps),
        in_specs=[tok, tok, tok, tok_dir, tok_dir, tok_dir, st],
        out_specs=[tok, st],
        out_shape=[jax.ShapeDtypeStruct((m, bw), F32), jax.ShapeDtypeStruct(s0.shape, F32)],
        scratch_shapes=[
            pltpu.VMEM((pairs, LANE, LANE), F32),
            pltpu.VMEM((nsub, pairs, 2 * c, LANE), BF16),
            pltpu.VMEM((nsub, pairs, c, LANE), F32),
            pltpu.VMEM((nsub, pairs, c, LANE), BF16),
            pltpu.VMEM((nsub, pairs, 2 * c, LANE), BF16),
            pltpu.VMEM((nsub, pairs, c, LANE), BF16),
            pltpu.VMEM((nsub, pairs, SUBLANE, LANE), F32),
        ],
        compiler_params=_cparams(("arbitrary", "arbitrary")),
    )(r, v, kk, lw, kka, kd, s0)


def _mixout_body(x_ref, gt_ref, ya_ref, yf_ref, yb_ref, bonus_ref, g_ref, lng_ref, lnb_ref, e_ref, wa_ref,
                 wb_ref, o_ref):
    e = e_ref[...]
    inv_n = 1.0 / HEAD_DIM
    y = yf_ref[...] + yb_ref[...]
    yc = y - _dot2(y, e) * inv_n
    var = _dot2(yc * yc, e) * inv_n
    yn = yc * lax.rsqrt(var + GN_EPS) * lng_ref[...] + lnb_ref[...]
    yb = ((yn + bonus_ref[...]) * g_ref[...]).astype(BF16)
    ox = _dot(ya_ref[...], wa_ref[...]) + _dot(yb, wb_ref[...])
    o_ref[...] = x_ref[...] + gt_ref[...] * ox


def _mix_out(x, gate, ya, y_fwd, y_bwd, bonus, g, ln_g, ln_b, e, w_out_a, w_out_b, *, rows_per_mod, tm=256):
    m, d = x.shape
    aw, bw = ya.shape[1], y_fwd.shape[1]
    tpb = rows_per_mod // tm
    full = lambda arr: pl.BlockSpec(arr.shape, lambda i: (0,) * arr.ndim)
    tok = lambda width: pl.BlockSpec((tm, width), lambda i: (i, 0))
    return pl.pallas_call(
        _mixout_body,
        grid=(m // tm,),
        in_specs=[
            tok(d), pl.BlockSpec((None, 1, d), lambda i: (i // tpb, 0, 0)),
            tok(aw), tok(bw), tok(bw), tok(bw), tok(bw),
            full(ln_g), full(ln_b), full(e), full(w_out_a), full(w_out_b),
        ],
        out_specs=tok(d),
        out_shape=jax.ShapeDtypeStruct((m, d), F32),
        compiler_params=_cparams(("arbitrary",)),
    )(x, gate, ya, y_fwd, y_bwd, bonus, g, ln_g, ln_b, e, w_out_a, w_out_b)


def kernel(x, c, ctx, c_ctx, ada_w, ada_b, norm_g, ffn_w1, ffn_w3, ffn_w2, ab_w_in, ab_w_out, gm_v_gain,
           gm_ws, gm_b, rw_mu, rw_w0, rw_w2, rw_a0, rw_a2, rw_g2, rw_k_k, rw_k_a, rw_r_k, rw_ln_g, rw_ln_b,
           sc_w_in, sc_conv, sc_w_out, final_g):
    bn, seq, d = x.shape
    ctx_len = ctx.shape[1]
    depth = ada_w.shape[0]
    n_mod = ada_w.shape[2] // d
    m_x, m_c = bn * seq, bn * ctx_len
    xs = x.reshape(m_x, d)
    cs = ctx.reshape(m_c, d)

    rows = _round_up(bn + 1, SUBLANE)
    c_rows = jnp.zeros((rows, d), F32).at[:bn].set(c).at[bn].set(c_ctx)
    mods = _modulation(c_rows, ada_w, ada_b).reshape(depth, rows, n_mod, d)

    for i in range(depth):
        even = i % 2 == 0
        ctx_after = any(j % 2 == 0 for j in range(i + 1, depth))
        use_ctx = cs is not None and (even or ctx_after)
        if not use_ctx:
            cs = None
        mx = [mods[i, :bn, j].reshape(bn, 1, d) for j in range(n_mod)]
        mc = [mods[i, bn:bn + 1, j].reshape(1, 1, d) for j in range(n_mod)]
        w1 = ffn_w1[i].astype(BF16)
        w3 = ffn_w3[i].astype(BF16)
        w2 = ffn_w2[i].astype(BF16)
        last_layer = i == depth - 1

        xs = _glu_block(xs, norm_g[i, 0], mx[0], mx[1], mx[2], [w1[0], w3[0]], w2[0], kind="swiglu",
                        coef=0.5, rows_per_mod=seq)
        if cs is not None:
            cs = _glu_block(cs, norm_g[i, 0], mc[0], mc[1], mc[2], [w1[0], w3[0]], w2[0], kind="swiglu",
                            coef=0.5, rows_per_mod=m_c)

        if even:
            e_idx = i // 2
            if ctx_after:
                raise NotImplementedError("context output of an even layer is only needed for depth > 2")
            xs = _ab_mixer(xs, cs, mx, mc, norm_g[i, 1], bn, seq, ctx_len, ab_w_in[e_idx], ab_w_out[e_idx],
                           gm_v_gain[e_idx], gm_ws[e_idx], gm_b[e_idx], rw_mu[e_idx], rw_w0[e_idx],
                           rw_w2[e_idx], rw_a0[e_idx], rw_a2[e_idx], rw_g2[e_idx], rw_k_k[e_idx],
                           rw_k_a[e_idx], rw_r_k[e_idx], rw_ln_g[e_idx], rw_ln_b[e_idx])
        else:
            o_idx = i // 2
            if ctx_after:
                raise NotImplementedError("context output of an odd layer is only needed for depth > 2")
            w_in = sc_w_in[o_idx].astype(BF16)
            ups = [w_in[:, 0:d], w_in[:, d:2 * d], w_in[:, 2 * d:3 * d]]
            xs = _glu_block(xs, norm_g[i, 1], mx[3], mx[4], mx[5], ups, sc_w_out[o_idx].astype(BF16),
                            kind="conv", coef=1.0, conv_w=sc_conv[o_idx], rows_per_mod=seq)
        cs = None

        xs = _glu_block(xs, norm_g[i, 2], mx[6], mx[7], mx[8], [w1[1], w3[1]], w2[1], kind="swiglu",
                        coef=0.5, final_g=final_g if last_layer else None, rows_per_mod=seq)
    return xs.reshape(bn, seq, d)


def _ab_mixer(xs, cs, mx, mc, g, bn, seq, ctx_len, w_in, w_out, v_gain, ws, b_s, mu, w0, w2, a0, a2, g2,
              k_k, k_a, r_k, ln_g, ln_b):
    m_x, d = xs.shape
    m_c = cs.shape[0]
    bw = k_k.shape[0]
    aw = v_gain.shape[0]
    a_cols = 2 * aw
    heads = bw // HEAD_DIM
    wl, al, gl = w2.shape[1], a2.shape[1], g2.shape[0]
    b_cols = w_in.shape[1] - a_cols
    pw = _round_up(b_cols, 512)
    gp = pw - (3 * bw + 2 * wl + 2 * al)

    w_in_bf = w_in.astype(BF16)
    w_in_a = w_in_bf[:, :a_cols]
    w_in_b = jnp.pad(w_in_bf[:, a_cols:], ((0, 0), (0, pw - b_cols)))

    mu_p = jnp.pad(mu, ((0, 0), (0, pw - b_cols)))
    zeros_w = jnp.zeros((wl, bw), F32)
    zeros_a = jnp.zeros((al, bw), F32)
    w2cat = jnp.concatenate([jnp.concatenate([w2[0], zeros_w], 1), jnp.concatenate([zeros_w, w2[1]], 1)], 0)
    a2cat = jnp.concatenate([jnp.concatenate([a2[0], zeros_a], 1), jnp.concatenate([zeros_a, a2[1]], 1)], 0)
    g2p = jnp.pad(g2, ((0, gp - gl), (0, 0)))
    head_id = jnp.arange(bw) // HEAD_DIM
    e = (head_id[:, None] == head_id[None, :]).astype(BF16)
    consts = (mu_p, w0.reshape(1, 2 * bw), w2cat.astype(BF16), a0.reshape(1, 2 * bw), a2cat.astype(BF16),
              g2p.astype(BF16), k_k.reshape(1, bw), k_a.reshape(1, bw), r_k.reshape(1, bw), e)

    pa_x = _prenorm_proj(xs, g, mx[3], mx[4], w_in_a, rows_per_mod=seq)
    pb_x = _prenorm_proj(xs, g, mx[3], mx[4], w_in_b, rows_per_mod=seq)
    pb_c = _prenorm_proj(cs, g, mc[3], mc[4], w_in_b, rows_per_mod=m_c)

    b_full = jnp.repeat(b_s.T, aw // ws.shape[0], axis=1)
    ya = _gmlp(pa_x, v_gain, ws.astype(BF16), b_full)

    fx = _rwkv_prep(pb_x, seq, consts)
    fc = _rwkv_prep(pb_c, ctx_len, consts)
    r_x, v_x, kk_x, g_x, bonus_x, lw_x, kka_x, kd_x = fx
    r_c, v_c, kk_c, _, _, lw_c, kka_c, kd_c = fc
    s_zero = jnp.zeros((bn, bw // LANE, LANE, LANE), F32)
    ys = []
    for reverse in (False, True):
        _, s_ctx = _chunk_scan(r_c, v_c, kk_c, lw_c, kka_c, kd_c, s_zero, bn=bn, reverse=reverse)
        y_dir, _ = _chunk_scan(r_x, v_x, kk_x, lw_x, kka_x, kd_x, s_ctx, bn=bn, reverse=reverse)
        ys.append(y_dir)

    w_out_bf = w_out.astype(BF16)
    return _mix_out(xs, mx[5], ya, ys[0], ys[1], bonus_x, g_x, ln_g.reshape(1, bw), ln_b.reshape(1, bw), e,
                    w_out_bf[:aw], w_out_bf[aw:], rows_per_mod=seq)
```

```python
import functools
import math

import jax
import jax.numpy as jnp
from jax import lax
from jax.experimental import pallas as pl
from jax.experimental.pallas import tpu as pltpu

F32 = jnp.float32
BF16 = jnp.bfloat16

GRID_W = 64
CHUNK = 128
HEAD_DIM = 64
RMS_EPS = 1e-6
LN_EPS = 1e-5
GN_EPS = 64e-5
KK_EPS = 1e-12
W_DECAY_SCALE = math.exp(-0.5)

LANE = 128
SUBLANE = 8
MXU_WIDTH = 256
VMEM_LIMIT = 56 * 1024 * 1024


def _cparams(sem):
    return pltpu.CompilerParams(dimension_semantics=sem, vmem_limit_bytes=VMEM_LIMIT)


def _round_up(n, m):
    return (n + m - 1) // m * m


def _dot(a, b):
    return jnp.dot(a, b, preferred_element_type=F32)


def _dot2(a, b):
    hi = a.astype(BF16)
    lo = (a - hi.astype(F32)).astype(BF16)
    return _dot(hi, b) + _dot(lo, b)


def _head_sums(x, e):
    blk = e.shape[0]
    parts = [_dot2(x[:, j:j + blk], e) for j in range(0, x.shape[1], blk)]
    return parts[0] if len(parts) == 1 else jnp.concatenate(parts, axis=1)


def _prenorm(x, g, shift, scale):
    ms = jnp.mean(x * x, axis=-1, keepdims=True)
    return (x * lax.rsqrt(ms + RMS_EPS) * g) * (1.0 + scale) + shift


def _mod_body(c_ref, w_ref, b_ref, o_ref):
    s = jax.nn.silu(c_ref[...]).astype(BF16)
    o_ref[...] = _dot(s, w_ref[...].astype(BF16)) + b_ref[...]


def _modulation(c_rows, ada_w, ada_b):
    depth, d, n = ada_w.shape
    tn = 1024 if n % 1024 == 0 else 512
    rows = c_rows.shape[0]
    return pl.pallas_call(
        _mod_body,
        grid=(depth, n // tn),
        in_specs=[
            pl.BlockSpec((rows, d), lambda l, j: (0, 0)),
            pl.BlockSpec((None, d, tn), lambda l, j: (l, 0, j)),
            pl.BlockSpec((None, 1, tn), lambda l, j: (l, 0, j)),
        ],
        out_specs=pl.BlockSpec((None, rows, tn), lambda l, j: (l, 0, j)),
        out_shape=jax.ShapeDtypeStruct((depth, rows, n), F32),
        compiler_params=_cparams(("arbitrary", "arbitrary")),
    )(c_rows, ada_w, ada_b.reshape(depth, 1, n))


def _glu_body(*refs, n_up, kind, coef, final, nf):
    x_ref, g_ref, sh_ref, sc_ref, gt_ref = refs[:5]
    up_refs = refs[5:5 + n_up]
    down_ref = refs[5 + n_up]
    idx = 6 + n_up
    cw_ref = fg_ref = None
    if kind == "conv":
        cw_ref = refs[idx]
        idx += 1
    if final:
        fg_ref = refs[idx]
        idx += 1
    o_ref, h_ref = refs[idx], refs[idx + 1]
    f = pl.program_id(1)

    @pl.when(f == 0)
    def _():
        h_ref[...] = _prenorm(x_ref[...], g_ref[...], sh_ref[...], sc_ref[...]).astype(BF16)
        o_ref[...] = jnp.zeros_like(o_ref)

    h = h_ref[...]
    ups = [_dot(h, u[...]) for u in up_refs]
    if kind == "swiglu":
        mid = jax.nn.silu(ups[0]) * ups[1]
    else:
        z = ups[1] * ups[2]
        rows = z.shape[0]
        col = lax.broadcasted_iota(jnp.int32, z.shape, 0) % GRID_W
        zp = jnp.where(col == 0, 0.0, pltpu.roll(z, 1, axis=0))
        zn = jnp.where(col == GRID_W - 1, 0.0, pltpu.roll(z, rows - 1, axis=0))
        cw = cw_ref[...]
        mid = ups[0] * (cw[0:1] * zp + cw[1:2] * z + cw[2:3] * zn)
    o_ref[...] += _dot(mid.astype(BF16), down_ref[...])

    @pl.when(f == nf - 1)
    def _():
        res = x_ref[...] + (coef * gt_ref[...]) * o_ref[...]
        if final:
            ms = jnp.mean(res * res, axis=-1, keepdims=True)
            res = res * lax.rsqrt(ms + RMS_EPS) * fg_ref[...]
        o_ref[...] = res


def _glu_block(x, g, shift, scale, gate, ups, down, *, kind, coef, conv_w=None, final_g=None,
               rows_per_mod, tm=1024, tf=512):
    m, d = x.shape
    fdim = down.shape[0]
    tm = min(tm, m, rows_per_mod)
    nf = fdim // tf
    tpb = rows_per_mod // tm
    mod_spec = pl.BlockSpec((None, 1, d), lambda i, f: (i // tpb, 0, 0))
    in_specs = [
        pl.BlockSpec((tm, d), lambda i, f: (i, 0), pipeline_mode=pl.Buffered(1)),
        pl.BlockSpec((1, d), lambda i, f: (0, 0)),
        mod_spec, mod_spec, mod_spec,
    ]
    args = [x, g.reshape(1, d), shift, scale, gate]
    for u in ups:
        in_specs.append(pl.BlockSpec((d, tf), lambda i, f: (0, f)))
        args.append(u)
    in_specs.append(pl.BlockSpec((tf, d), lambda i, f: (f, 0)))
    args.append(down)
    if kind == "conv":
        in_specs.append(pl.BlockSpec((3, tf), lambda i, f: (0, f)))
        args.append(conv_w)
    if final_g is not None:
        in_specs.append(pl.BlockSpec((1, d), lambda i, f: (0, 0)))
        args.append(final_g.reshape(1, d))
    body = functools.partial(_glu_body, n_up=len(ups), kind=kind, coef=coef,
                             final=final_g is not None, nf=nf)
    return pl.pallas_call(
        body,
        grid=(m // tm, nf),
        in_specs=in_specs,
        out_specs=pl.BlockSpec((tm, d), lambda i, f: (i, 0)),
        out_shape=jax.ShapeDtypeStruct((m, d), F32),
        scratch_shapes=[pltpu.VMEM((tm, d), BF16)],
        compiler_params=_cparams(("arbitrary", "arbitrary")),
    )(*args)


def _proj_body(x_ref, g_ref, sh_ref, sc_ref, w_ref, o_ref):
    h = _prenorm(x_ref[...], g_ref[...], sh_ref[...], sc_ref[...]).astype(BF16)
    o_ref[...] = _dot(h, w_ref[...])


def _prenorm_proj(x, g, shift, scale, w, *, rows_per_mod, tm=512):
    m, d = x.shape
    n = w.shape[1]
    tm = min(tm, m, rows_per_mod)
    tpb = rows_per_mod // tm
    mod_spec = pl.BlockSpec((None, 1, d), lambda i: (i // tpb, 0, 0))
    return pl.pallas_call(
        _proj_body,
        grid=(m // tm,),
        in_specs=[
            pl.BlockSpec((tm, d), lambda i: (i, 0)),
            pl.BlockSpec((1, d), lambda i: (0, 0)),
            mod_spec, mod_spec,
            pl.BlockSpec((d, n), lambda i: (0, 0), pipeline_mode=pl.Buffered(1)),
        ],
        out_specs=pl.BlockSpec((tm, n), lambda i: (i, 0)),
        out_shape=jax.ShapeDtypeStruct((m, n), F32),
        compiler_params=_cparams(("arbitrary",)),
    )(x, g.reshape(1, d), shift, scale, w)


def _gmlp_body(p_ref, gain_ref, ws_ref, b_ref, o_ref, *, aw, groups, n_chunks):
    gd = aw // groups
    for c in range(n_chunks):
        rows = pl.ds(c * CHUNK, CHUNK)
        u = jax.nn.gelu(p_ref[rows, 0:aw])
        v = jax.nn.gelu(p_ref[rows, aw:2 * aw])
        vm = jnp.mean(v, axis=-1, keepdims=True)
        vc = v - vm
        vv = jnp.mean(vc * vc, axis=-1, keepdims=True)
        vn = (vc * lax.rsqrt(vv + LN_EPS) * gain_ref[...]).astype(BF16)
        for gi in range(groups):
            cols = slice(gi * gd, (gi + 1) * gd)
            vs = _dot(ws_ref[gi], vn[:, cols]) + b_ref[:, cols]
            o_ref[rows, cols] = (u[:, cols] * vs).astype(o_ref.dtype)


def _gmlp(pa, gain, ws, b_full, *, tm=512):
    m, two_aw = pa.shape
    aw = two_aw // 2
    groups = ws.shape[0]
    tm = min(tm, m)
    body = functools.partial(_gmlp_body, aw=aw, groups=groups, n_chunks=tm // CHUNK)
    return pl.pallas_call(
        body,
        grid=(m // tm,),
        in_specs=[
            pl.BlockSpec((tm, two_aw), lambda i: (i, 0)),
            pl.BlockSpec((1, aw), lambda i: (0, 0)),
            pl.BlockSpec((groups, CHUNK, CHUNK), lambda i: (0, 0, 0)),
            pl.BlockSpec((CHUNK, aw), lambda i: (0, 0)),
        ],
        out_specs=pl.BlockSpec((tm, aw), lambda i: (i, 0)),
        out_shape=jax.ShapeDtypeStruct((m, aw), BF16),
        compiler_params=_cparams(("arbitrary",)),
    )(pa, gain.reshape(1, aw), ws, b_full)


def _prep_body(p_ref, pp_ref, pn_ref, mu_ref, w0_ref, w2_ref, a0_ref, a2_ref, g2_ref, kk_ref, ka_ref,
               rk_ref, e_ref, r_o, v_o, kk_o, g_o, bonus_o, lw_o, kka_o, kd_o, *, bw, wl2, al2, seq, tm):
    i = pl.program_id(0)
    p = p_ref[...]
    row = lax.broadcasted_iota(jnp.int32, p.shape, 0)
    first = (i * tm) % seq == 0
    last = ((i + 1) * tm) % seq == 0
    prev_row = jnp.where(first, 0.0, pp_ref[SUBLANE - 1:SUBLANE, :])
    next_row = jnp.where(last, 0.0, pn_ref[0:1, :])
    prev = jnp.where(row == 0, prev_row, pltpu.roll(p, 1, axis=0))
    nxt = jnp.where(row == tm - 1, next_row, pltpu.roll(p, tm - 1, axis=0))
    mu = mu_ref[...]
    p = p + mu[0:1] * (prev - p) + mu[1:2] * (nxt - p)

    r = p[:, 0:bw]
    k = p[:, bw:2 * bw]
    v = p[:, 2 * bw:3 * bw]
    o = 3 * bw
    wd = jnp.tanh(p[:, o:o + wl2]).astype(BF16)
    ad = p[:, o + wl2:o + wl2 + al2].astype(BF16)
    gdn = jax.nn.sigmoid(p[:, o + wl2 + al2:]).astype(BF16)

    lw = -W_DECAY_SCALE * jax.nn.sigmoid(w0_ref[...] + _dot(wd, w2_ref[...]))
    a = jax.nn.sigmoid(a0_ref[...] + _dot(ad, a2_ref[...]))
    g = _dot(gdn, g2_ref[...])

    e = e_ref[...]
    kq = k * kk_ref[...]
    kk = kq * lax.rsqrt(_head_sums(kq * kq, e) + KK_EPS)
    ka = ka_ref[...]
    kd0 = k * (1.0 + (a[:, 0:bw] - 1.0) * ka)
    kd1 = k * (1.0 + (a[:, bw:] - 1.0) * ka)
    kb = 0.5 * (kd0 + kd1)
    bonus = _head_sums(r * kb * rk_ref[...], e) * v

    r_o[...] = r
    v_o[...] = v
    kk_o[...] = kk
    g_o[...] = g
    bonus_o[...] = bonus
    lw_o[...] = lw
    kka_o[:, 0:bw] = kk * a[:, 0:bw]
    kka_o[:, bw:] = kk * a[:, bw:]
    kd_o[:, 0:bw] = kd0
    kd_o[:, bw:] = kd1


def _rwkv_prep(pb, seq, consts, *, tm=256):
    m, pw = pb.shape
    (mu, w0, w2cat, a0, a2cat, g2p, k_k, k_a, r_k, e) = consts
    bw = k_k.shape[1]
    wl2, al2 = w2cat.shape[0], a2cat.shape[0]
    tm = min(tm, seq)
    nsub = tm // SUBLANE
    last_blk = m // SUBLANE - 1
    full = lambda arr: pl.BlockSpec(arr.shape, lambda i: (0,) * arr.ndim)
    body = functools.partial(_prep_body, bw=bw, wl2=wl2, al2=al2, seq=seq, tm=tm)
    one = jax.ShapeDtypeStruct((m, bw), F32)
    two = jax.ShapeDtypeStruct((m, 2 * bw), F32)
    o1 = pl.BlockSpec((tm, bw), lambda i: (i, 0))
    o2 = pl.BlockSpec((tm, 2 * bw), lambda i: (i, 0))
    return pl.pallas_call(
        body,
        grid=(m // tm,),
        in_specs=[
            pl.BlockSpec((tm, pw), lambda i: (i, 0)),
            pl.BlockSpec((SUBLANE, pw), lambda i: (jnp.maximum(i * nsub - 1, 0), 0)),
            pl.BlockSpec((SUBLANE, pw), lambda i: (jnp.minimum((i + 1) * nsub, last_blk), 0)),
            full(mu), full(w0), full(w2cat), full(a0), full(a2cat), full(g2p), full(k_k), full(k_a),
            full(r_k), full(e),
        ],
        out_specs=[o1, o1, o1, o1, o1, o2, o2, o2],
        out_shape=[one, one, one, one, one, two, two, two],
        compiler_params=_cparams(("arbitrary",)),
    )(pb, pb, pb, mu, w0, w2cat, a0, a2cat, g2p, k_k, k_a, r_k, e)


SCAN_CHUNK = 64
INV_BLOCK = 16


def _dot_nt(a, b):
    return lax.dot_general(a, b, (((1,), (1,)), ((), ())), preferred_element_type=F32)


def _dot_tn(a, b):
    return lax.dot_general(a, b, (((0,), (0,)), ((), ())), preferred_element_type=F32)


def _pair_blockdiag(x, lane_lo):
    return jnp.concatenate([jnp.where(lane_lo, x, 0.0), jnp.where(lane_lo, 0.0, x)], axis=0).astype(BF16)


def _chunk_scan_body(r_ref, v_ref, kk_ref, lw_ref, kka_ref, kd_ref, s0_ref, y_ref, sout_ref,
                     s_ref, qw_ref, u0_ref, arb_ref, pe_ref, vb_ref, elt_ref, *, reverse, nsub, pairs, group):
    c = SCAN_CHUNK
    step = pl.program_id(1)
    bf = lambda t: t.astype(BF16)

    @pl.when(step == 0)
    def _():
        s_ref[...] = s0_ref[...]

    ri = lax.broadcasted_iota(jnp.int32, (c, c), 0)
    ci = lax.broadcasted_iota(jnp.int32, (c, c), 1)
    tri = ((ri <= ci) if reverse else (ri >= ci)).astype(BF16)
    rp = lax.broadcasted_iota(jnp.int32, (c, LANE), 0)
    cp = lax.broadcasted_iota(jnp.int32, (c, LANE), 1) % c
    incl_p = (rp <= cp) if reverse else (rp >= cp)
    strict_p = (rp < cp) if reverse else (rp > cp)
    blk_p = (rp // INV_BLOCK) == (cp // INV_BLOCK)
    eye_p = (rp == cp).astype(F32)
    lane_lo = lax.broadcasted_iota(jnp.int32, (c, LANE), 1) < HEAD_DIM
    bi = lax.broadcasted_iota(jnp.int32, (LANE, LANE), 0) // HEAD_DIM
    bj = lax.broadcasted_iota(jnp.int32, (LANE, LANE), 1) // HEAD_DIM
    same_head = bi == bj
    bd = lambda t: _pair_blockdiag(t, lane_lo)
    prange = range(pairs)
    lanes = [pl.ds(p * LANE, LANE) for p in prange]

    def phase_a(jg, carry):
        streams = [(jg * group + gi, p) for gi in range(group) for p in prange]
        sr = range(len(streams))
        rows = [pl.ds(pl.multiple_of(j * c, c), c) for j, _ in streams]
        cols = [lanes[p] for _, p in streams]
        lw = [lw_ref[rows[s], cols[s]] for s in sr]
        lc = []
        for s in sr:
            hi = bf(lw[s])
            lc.append(_dot(tri, hi) + _dot(tri, bf(lw[s] - hi.astype(F32))))
        qq, qk_bd, qr, pk_bd, pb_bd, v_bd = [], [], [], [], [], []
        for s, (j, p) in enumerate(streams):
            ltot = lc[s][0:1] if reverse else lc[s][c - 1:c]
            e_neg = jnp.exp(-lc[s])
            e_end = jnp.exp(ltot - lc[s])
            kd = kd_ref[rows[s], cols[s]]
            kka = kka_ref[rows[s], cols[s]]
            v = v_ref[rows[s], cols[s]]
            qk = kk_ref[rows[s], cols[s]] * jnp.exp(lc[s] - lw[s])
            qr.append(r_ref[rows[s], cols[s]] * jnp.exp(lc[s]))
            qq.append(bf(jnp.concatenate([qk, qr[s]], axis=0)))
            qk_bd.append(bd(qk))
            pk_bd.append(bd(kd * e_neg))
            pb_bd.append(bd(kka * e_neg))
            v_bd.append(bd(v))
            pe_ref[j, p] = bf(jnp.concatenate([kd * e_end, -(kka * e_end)], axis=0))
            vb_ref[j, p] = bf(v)
            elt_ref[j, p] = jnp.broadcast_to(jnp.exp(ltot), (SUBLANE, LANE))
        g1 = [_dot_nt(qq[s], pk_bd[s]) for s in sr]
        g2 = [_dot_nt(qq[s], pb_bd[s]) for s in sr]
        akk = [bf(jnp.where(strict_p, g1[s][:c], 0.0)) for s in sr]
        ark = [bf(jnp.where(incl_p, g1[s][c:], 0.0)) for s in sr]
        nmat = [jnp.where(strict_p, g2[s][:c], 0.0) for s in sr]
        for s, (j, p) in enumerate(streams):
            arb_ref[j, p] = bf(jnp.where(incl_p, g2[s][c:], 0.0))
        av = [_dot(akk[s], v_bd[s]) for s in sr]
        for s in sr:
            y_ref[rows[s], cols[s]] = _dot(ark[s], v_bd[s])
        nd = [jnp.where(blk_p, nmat[s], 0.0) for s in sr]
        lo_bd = [bd(nmat[s] - nd[s]) for s in sr]
        mpow = [-nd[s] for s in sr]
        dinv = [eye_p + mpow[s] for s in sr]
        for _ in range(int(math.log2(INV_BLOCK)) - 1):
            mpow = [_dot(bf(mpow[s]), bd(mpow[s])) for s in sr]
            dinv = [dinv[s] + _dot(bf(dinv[s]), bd(mpow[s])) for s in sr]
        dinv_b = [bf(dinv[s]) for s in sr]
        x1 = [_dot(dinv_b[s], lo_bd[s]) for s in sr]
        x1b = [bf(x1[s]) for s in sr]
        acc = [eye_p - x1[s] for s in sr]
        xp = x1
        for k in range(2, SCAN_CHUNK // INV_BLOCK):
            xp = [_dot(x1b[s], bd(xp[s])) for s in sr]
            acc = [acc[s] + xp[s] if k % 2 == 0 else acc[s] - xp[s] for s in sr]
        tinv = [bf(_dot(bf(acc[s]), bd(dinv[s]))) for s in sr]
        wt = [_dot(tinv[s], qk_bd[s]) for s in sr]
        for s, (j, p) in enumerate(streams):
            u0_ref[j, p] = _dot(tinv[s], bd(av[s]))
            qw_ref[j, p] = bf(jnp.concatenate([wt[s], qr[s]], axis=0))
        return carry

    lax.fori_loop(0, nsub // group, phase_a, 0)

    def phase_b(j, carry):
        jj = (nsub - 1 - j) if reverse else j
        rows = pl.ds(pl.multiple_of(jj * c, c), c)
        s = [s_ref[p] for p in prange]
        xs = [_dot_nt(qw_ref[jj, p], bf(s[p])) for p in prange]
        u = [u0_ref[jj, p] + xs[p][:c] for p in prange]
        au = [_dot(arb_ref[jj, p], bd(u[p])) for p in prange]
        for p in prange:
            y_ref[rows, lanes[p]] = y_ref[rows, lanes[p]] + xs[p][c:] - au[p]
        upd = [_dot_tn(jnp.concatenate([vb_ref[jj, p], bf(u[p])], axis=0), pe_ref[jj, p]) for p in prange]
        for p in prange:
            s_ref[p] = s[p] * elt_ref[jj, p][0:1] + jnp.where(same_head, upd[p], 0.0)
        return carry

    lax.fori_loop(0, nsub, phase_b, 0)

    @pl.when(step == pl.num_programs(1) - 1)
    def _():
        sout_ref[...] = s_ref[...]


def _chunk_scan(r, v, kk, lw, kka, kd, s0, *, bn, reverse, tt=256):
    m, bw = r.shape
    seq = m // bn
    tt = min(tt, seq)
    steps = seq // tt
    pairs = bw // LANE
    d = 1 if reverse else 0

    def row_blk(b, s):
        return b * steps + (steps - 1 - s if reverse else s)

    tok = pl.BlockSpec((tt, bw), lambda b, s: (row_blk(b, s), 0))
    tok_dir = pl.BlockSpec((tt, bw), lambda b, s: (row_blk(b, s), d))
    st = pl.BlockSpec((None, pairs, LANE, LANE), lambda b, s: (b, 0, 0, 0))
    c = SCAN_CHUNK
    nsub = tt // c
    group = 2 if nsub % 2 == 0 else 1
    body = functools.partial(_chunk_scan_body, reverse=reverse, nsub=nsub, pairs=pairs, group=group)
    return pl.pallas_call(
        body,
        grid=(bn, steps),
        in_specs=[tok, tok, tok, tok_dir, tok_dir, tok_dir, st],
        out_specs=[tok, st],
        out_shape=[jax.ShapeDtypeStruct((m, bw), F32), jax.ShapeDtypeStruct(s0.shape, F32)],
        scratch_shapes=[
            pltpu.VMEM((pairs, LANE, LANE), F32),
            pltpu.VMEM((nsub, pairs, 2 * c, LANE), BF16),
            pltpu.VMEM((nsub, pairs, c, LANE), F32),
            pltpu.VMEM((nsub, pairs, c, LANE), BF16),
            pltpu.VMEM((nsub, pairs, 2 * c, LANE), BF16),
            pltpu.VMEM((nsub, pairs, c, LANE), BF16),
            pltpu.VMEM((nsub, pairs, SUBLANE, LANE), F32),
        ],
        compiler_params=_cparams(("arbitrary", "arbitrary")),
    )(r, v, kk, lw, kka, kd, s0)


def _mixout_body(x_ref, gt_ref, ya_ref, yf_ref, yb_ref, bonus_ref, g_ref, lng_ref, lnb_ref, e_ref, wa_ref,
                 wb_ref, o_ref):
    e = e_ref[...]
    inv_n = 1.0 / HEAD_DIM
    y = yf_ref[...] + yb_ref[...]
    yc = y - _head_sums(y, e) * inv_n
    var = _head_sums(yc * yc, e) * inv_n
    yn = yc * lax.rsqrt(var + GN_EPS) * lng_ref[...] + lnb_ref[...]
    yb = ((yn + bonus_ref[...]) * g_ref[...]).astype(BF16)
    ox = _dot(ya_ref[...], wa_ref[...]) + _dot(yb, wb_ref[...])
    o_ref[...] = x_ref[...] + gt_ref[...] * ox


def _mix_out(x, gate, ya, y_fwd, y_bwd, bonus, g, ln_g, ln_b, e, w_out_a, w_out_b, *, rows_per_mod, tm=256):
    m, d = x.shape
    aw, bw = ya.shape[1], y_fwd.shape[1]
    tpb = rows_per_mod // tm
    full = lambda arr: pl.BlockSpec(arr.shape, lambda i: (0,) * arr.ndim)
    tok = lambda width: pl.BlockSpec((tm, width), lambda i: (i, 0))
    return pl.pallas_call(
        _mixout_body,
        grid=(m // tm,),
        in_specs=[
            tok(d), pl.BlockSpec((None, 1, d), lambda i: (i // tpb, 0, 0)),
            tok(aw), tok(bw), tok(bw), tok(bw), tok(bw),
            full(ln_g), full(ln_b), full(e), full(w_out_a), full(w_out_b),
        ],
        out_specs=tok(d),
        out_shape=jax.ShapeDtypeStruct((m, d), F32),
        compiler_params=_cparams(("arbitrary",)),
    )(x, gate, ya, y_fwd, y_bwd, bonus, g, ln_g, ln_b, e, w_out_a, w_out_b)


def kernel(x, c, ctx, c_ctx, ada_w, ada_b, norm_g, ffn_w1, ffn_w3, ffn_w2, ab_w_in, ab_w_out, gm_v_gain,
           gm_ws, gm_b, rw_mu, rw_w0, rw_w2, rw_a0, rw_a2, rw_g2, rw_k_k, rw_k_a, rw_r_k, rw_ln_g, rw_ln_b,
           sc_w_in, sc_conv, sc_w_out, final_g):
    bn, seq, d = x.shape
    ctx_len = ctx.shape[1]
    depth = ada_w.shape[0]
    n_mod = ada_w.shape[2] // d
    m_x, m_c = bn * seq, bn * ctx_len
    xs = x.reshape(m_x, d)
    cs = ctx.reshape(m_c, d)

    rows = _round_up(bn + 1, SUBLANE)
    c_rows = jnp.zeros((rows, d), F32).at[:bn].set(c).at[bn].set(c_ctx)
    mods = _modulation(c_rows, ada_w, ada_b).reshape(depth, rows, n_mod, d)

    for i in range(depth):
        even = i % 2 == 0
        ctx_after = any(j % 2 == 0 for j in range(i + 1, depth))
        use_ctx = cs is not None and (even or ctx_after)
        if not use_ctx:
            cs = None
        mx = [mods[i, :bn, j].reshape(bn, 1, d) for j in range(n_mod)]
        mc = [mods[i, bn:bn + 1, j].reshape(1, 1, d) for j in range(n_mod)]
        w1 = ffn_w1[i].astype(BF16)
        w3 = ffn_w3[i].astype(BF16)
        w2 = ffn_w2[i].astype(BF16)
        last_layer = i == depth - 1

        xs = _glu_block(xs, norm_g[i, 0], mx[0], mx[1], mx[2], [w1[0], w3[0]], w2[0], kind="swiglu",
                        coef=0.5, rows_per_mod=seq)
        if cs is not None:
            cs = _glu_block(cs, norm_g[i, 0], mc[0], mc[1], mc[2], [w1[0], w3[0]], w2[0], kind="swiglu",
                            coef=0.5, rows_per_mod=m_c)

        if even:
            e_idx = i // 2
            if ctx_after:
                raise NotImplementedError("context output of an even layer is only needed for depth > 2")
            xs = _ab_mixer(xs, cs, mx, mc, norm_g[i, 1], bn, seq, ctx_len, ab_w_in[e_idx], ab_w_out[e_idx],
                           gm_v_gain[e_idx], gm_ws[e_idx], gm_b[e_idx], rw_mu[e_idx], rw_w0[e_idx],
                           rw_w2[e_idx], rw_a0[e_idx], rw_a2[e_idx], rw_g2[e_idx], rw_k_k[e_idx],
                           rw_k_a[e_idx], rw_r_k[e_idx], rw_ln_g[e_idx], rw_ln_b[e_idx])
        else:
            o_idx = i // 2
            if ctx_after:
                raise NotImplementedError("context output of an odd layer is only needed for depth > 2")
            w_in = sc_w_in[o_idx].astype(BF16)
            ups = [w_in[:, 0:d], w_in[:, d:2 * d], w_in[:, 2 * d:3 * d]]
            xs = _glu_block(xs, norm_g[i, 1], mx[3], mx[4], mx[5], ups, sc_w_out[o_idx].astype(BF16),
                            kind="conv", coef=1.0, conv_w=sc_conv[o_idx], rows_per_mod=seq, tm=512)
        cs = None

        xs = _glu_block(xs, norm_g[i, 2], mx[6], mx[7], mx[8], [w1[1], w3[1]], w2[1], kind="swiglu",
                        coef=0.5, final_g=final_g if last_layer else None, rows_per_mod=seq)
    return xs.reshape(bn, seq, d)


def _ab_mixer(xs, cs, mx, mc, g, bn, seq, ctx_len, w_in, w_out, v_gain, ws, b_s, mu, w0, w2, a0, a2, g2,
              k_k, k_a, r_k, ln_g, ln_b):
    m_x, d = xs.shape
    m_c = cs.shape[0]
    bw = k_k.shape[0]
    aw = v_gain.shape[0]
    a_cols = 2 * aw
    heads = bw // HEAD_DIM
    wl, al, gl = w2.shape[1], a2.shape[1], g2.shape[0]
    b_cols = w_in.shape[1] - a_cols
    pw = _round_up(b_cols, 512)
    gp = pw - (3 * bw + 2 * wl + 2 * al)

    w_in_bf = w_in.astype(BF16)
    w_in_a = w_in_bf[:, :a_cols]
    w_in_b = jnp.pad(w_in_bf[:, a_cols:], ((0, 0), (0, pw - b_cols)))

    mu_p = jnp.pad(mu, ((0, 0), (0, pw - b_cols)))
    zeros_w = jnp.zeros((wl, bw), F32)
    zeros_a = jnp.zeros((al, bw), F32)
    w2cat = jnp.concatenate([jnp.concatenate([w2[0], zeros_w], 1), jnp.concatenate([zeros_w, w2[1]], 1)], 0)
    a2cat = jnp.concatenate([jnp.concatenate([a2[0], zeros_a], 1), jnp.concatenate([zeros_a, a2[1]], 1)], 0)
    g2p = jnp.pad(g2, ((0, gp - gl), (0, 0)))
    head_id = jnp.arange(min(bw, MXU_WIDTH)) // HEAD_DIM
    e = (head_id[:, None] == head_id[None, :]).astype(BF16)
    consts = (mu_p, w0.reshape(1, 2 * bw), w2cat.astype(BF16), a0.reshape(1, 2 * bw), a2cat.astype(BF16),
              g2p.astype(BF16), k_k.reshape(1, bw), k_a.reshape(1, bw), r_k.reshape(1, bw), e)

    pa_x = _prenorm_proj(xs, g, mx[3], mx[4], w_in_a, rows_per_mod=seq)
    pb_x = _prenorm_proj(xs, g, mx[3], mx[4], w_in_b, rows_per_mod=seq)
    pb_c = _prenorm_proj(cs, g, mc[3], mc[4], w_in_b, rows_per_mod=m_c)

    b_full = jnp.repeat(b_s.T, aw // ws.shape[0], axis=1)
    ya = _gmlp(pa_x, v_gain, ws.astype(BF16), b_full)

    fx = _rwkv_prep(pb_x, seq, consts)
    fc = _rwkv_prep(pb_c, ctx_len, consts)
    r_x, v_x, kk_x, g_x, bonus_x, lw_x, kka_x, kd_x = fx
    r_c, v_c, kk_c, _, _, lw_c, kka_c, kd_c = fc
    s_zero = jnp.zeros((bn, bw // LANE, LANE, LANE), F32)
    ys = []
    for reverse in (False, True):
        _, s_ctx = _chunk_scan(r_c, v_c, kk_c, lw_c, kka_c, kd_c, s_zero, bn=bn, reverse=reverse)
        y_dir, _ = _chunk_scan(r_x, v_x, kk_x, lw_x, kka_x, kd_x, s_ctx, bn=bn, reverse=reverse)
        ys.append(y_dir)

    w_out_bf = w_out.astype(BF16)
    return _mix_out(xs, mx[5], ya, ys[0], ys[1], bonus_x, g_x, ln_g.reshape(1, bw), ln_b.reshape(1, bw), e,
                    w_out_bf[:aw], w_out_bf[aw:], rows_per_mod=seq)
```

```python
import functools
import math

import jax
import jax.numpy as jnp
from jax import lax
from jax.experimental import pallas as pl
from jax.experimental.pallas import tpu as pltpu

F32 = jnp.float32
BF16 = jnp.bfloat16

GRID_W = 64
CHUNK = 128
HEAD_DIM = 64
RMS_EPS = 1e-6
LN_EPS = 1e-5
GN_EPS = 64e-5
KK_EPS = 1e-12
W_DECAY_SCALE = math.exp(-0.5)

LANE = 128
SUBLANE = 8
MXU_WIDTH = 256
VMEM_LIMIT = 56 * 1024 * 1024


def _cparams(sem):
    return pltpu.CompilerParams(dimension_semantics=sem, vmem_limit_bytes=VMEM_LIMIT)


def _round_up(n, m):
    return (n + m - 1) // m * m


def _dot(a, b):
    return jnp.dot(a, b, preferred_element_type=F32)


def _dot2(a, b):
    hi = a.astype(BF16)
    lo = (a - hi.astype(F32)).astype(BF16)
    return _dot(hi, b) + _dot(lo, b)


def _head_sums(x, e):
    blk = e.shape[0]
    parts = [_dot2(x[:, j:j + blk], e) for j in range(0, x.shape[1], blk)]
    return parts[0] if len(parts) == 1 else jnp.concatenate(parts, axis=1)


ROW_BLOCK = 16
ROW_UNROLL = 8


def _prenorm_rows(x_ref, g, shift, scale, h_ref, *, src_row=0, dst_row=0, n_rows=None):
    n_rows = x_ref.shape[0] if n_rows is None else n_rows
    gs = g * (1.0 + scale)

    def block(i, carry):
        off = pl.multiple_of(i * ROW_BLOCK, ROW_BLOCK)
        x = x_ref[pl.ds(src_row + off, ROW_BLOCK), :]
        ms = jnp.mean(x * x, axis=-1, keepdims=True)
        h_ref[pl.ds(dst_row + off, ROW_BLOCK), :] = (x * lax.rsqrt(ms + RMS_EPS) * gs + shift).astype(h_ref.dtype)
        return carry

    n_blocks = n_rows // ROW_BLOCK
    lax.fori_loop(0, n_blocks, block, 0, unroll=math.gcd(n_blocks, ROW_UNROLL))


def _mod_body(c_ref, w_ref, b_ref, o_ref):
    s = jax.nn.silu(c_ref[...]).astype(BF16)
    o_ref[...] = _dot(s, w_ref[...].astype(BF16)) + b_ref[...]


def _modulation(c_rows, ada_w, ada_b):
    depth, d, n = ada_w.shape
    tn = 1024 if n % 1024 == 0 else 512
    rows = c_rows.shape[0]
    return pl.pallas_call(
        _mod_body,
        grid=(depth, n // tn),
        in_specs=[
            pl.BlockSpec((rows, d), lambda l, j: (0, 0)),
            pl.BlockSpec((None, d, tn), lambda l, j: (l, 0, j)),
            pl.BlockSpec((None, 1, tn), lambda l, j: (l, 0, j)),
        ],
        out_specs=pl.BlockSpec((None, rows, tn), lambda l, j: (l, 0, j)),
        out_shape=jax.ShapeDtypeStruct((depth, rows, n), F32),
        compiler_params=_cparams(("arbitrary", "arbitrary")),
    )(c_rows, ada_w, ada_b.reshape(depth, 1, n))


def _glu_body(*refs, n_up, kind, coef, final, nf):
    x_ref, g_ref, sh_ref, sc_ref, gt_ref = refs[:5]
    up_refs = refs[5:5 + n_up]
    down_ref = refs[5 + n_up]
    idx = 6 + n_up
    cw_ref = fg_ref = None
    if kind == "conv":
        cw_ref = refs[idx]
        idx += 1
    if final:
        fg_ref = refs[idx]
        idx += 1
    o_ref, h_ref = refs[idx], refs[idx + 1]
    f = pl.program_id(1)

    @pl.when(f == 0)
    def _():
        _prenorm_rows(x_ref, g_ref[...], sh_ref[...], sc_ref[...], h_ref)
        o_ref[...] = jnp.zeros_like(o_ref)

    h = h_ref[...]
    ups = [_dot(h, u[...]) for u in up_refs]
    if kind == "swiglu":
        mid = jax.nn.silu(ups[0]) * ups[1]
    else:
        z = ups[1] * ups[2]
        rows = z.shape[0]
        col = lax.broadcasted_iota(jnp.int32, z.shape, 0) % GRID_W
        zp = jnp.where(col == 0, 0.0, pltpu.roll(z, 1, axis=0))
        zn = jnp.where(col == GRID_W - 1, 0.0, pltpu.roll(z, rows - 1, axis=0))
        cw = cw_ref[...]
        mid = ups[0] * (cw[0:1] * zp + cw[1:2] * z + cw[2:3] * zn)
    o_ref[...] += _dot(mid.astype(BF16), down_ref[...])

    @pl.when(f == nf - 1)
    def _():
        res = x_ref[...] + (coef * gt_ref[...]) * o_ref[...]
        if final:
            ms = jnp.mean(res * res, axis=-1, keepdims=True)
            res = res * lax.rsqrt(ms + RMS_EPS) * fg_ref[...]
        o_ref[...] = res


def _glu_block(x, g, shift, scale, gate, ups, down, *, kind, coef, conv_w=None, final_g=None,
               rows_per_mod, tm=1024, tf=512):
    m, d = x.shape
    fdim = down.shape[0]
    tm = min(tm, m, rows_per_mod)
    nf = fdim // tf
    tpb = rows_per_mod // tm
    mod_spec = pl.BlockSpec((None, 1, d), lambda i, f: (i // tpb, 0, 0))
    x_mode = dict(pipeline_mode=pl.Buffered(1)) if tm > 512 else {}
    in_specs = [
        pl.BlockSpec((tm, d), lambda i, f: (i, 0), **x_mode),
        pl.BlockSpec((1, d), lambda i, f: (0, 0)),
        mod_spec, mod_spec, mod_spec,
    ]
    args = [x, g.reshape(1, d), shift, scale, gate]
    for u in ups:
        in_specs.append(pl.BlockSpec((d, tf), lambda i, f: (0, f)))
        args.append(u)
    in_specs.append(pl.BlockSpec((tf, d), lambda i, f: (f, 0)))
    args.append(down)
    if kind == "conv":
        in_specs.append(pl.BlockSpec((3, tf), lambda i, f: (0, f)))
        args.append(conv_w)
    if final_g is not None:
        in_specs.append(pl.BlockSpec((1, d), lambda i, f: (0, 0)))
        args.append(final_g.reshape(1, d))
    body = functools.partial(_glu_body, n_up=len(ups), kind=kind, coef=coef,
                             final=final_g is not None, nf=nf)
    return pl.pallas_call(
        body,
        grid=(m // tm, nf),
        in_specs=in_specs,
        out_specs=pl.BlockSpec((tm, d), lambda i, f: (i, 0)),
        out_shape=jax.ShapeDtypeStruct((m, d), F32),
        scratch_shapes=[pltpu.VMEM((tm, d), BF16)],
        compiler_params=_cparams(("arbitrary", "arbitrary")),
    )(*args)


def _gmlp_body(x_ref, g_ref, sh_ref, sc_ref, w_ref, gain_ref, ws_ref, b_ref, o_ref, h_ref, p_ref, *,
               aw, groups, n_chunks):
    _prenorm_rows(x_ref, g_ref[...], sh_ref[...], sc_ref[...], h_ref)
    p_ref[...] = _dot(h_ref[...], w_ref[...])
    gd = aw // groups
    for c in range(n_chunks):
        rows = pl.ds(c * CHUNK, CHUNK)
        u = jax.nn.gelu(p_ref[rows, 0:aw])
        v = jax.nn.gelu(p_ref[rows, aw:2 * aw])
        vm = jnp.mean(v, axis=-1, keepdims=True)
        vc = v - vm
        vv = jnp.mean(vc * vc, axis=-1, keepdims=True)
        vn = (vc * lax.rsqrt(vv + LN_EPS) * gain_ref[...]).astype(BF16)
        for gi in range(groups):
            cols = slice(gi * gd, (gi + 1) * gd)
            vs = _dot(ws_ref[gi], vn[:, cols]) + b_ref[:, cols]
            o_ref[rows, cols] = (u[:, cols] * vs).astype(o_ref.dtype)


def _gmlp_branch(x, g, shift, scale, w_a, gain, ws, b_full, *, rows_per_mod, tm=512):
    m, d = x.shape
    two_aw = w_a.shape[1]
    aw = two_aw // 2
    groups = ws.shape[0]
    tm = min(tm, m, rows_per_mod)
    tpb = rows_per_mod // tm
    mod_spec = pl.BlockSpec((None, 1, d), lambda i: (i // tpb, 0, 0))
    body = functools.partial(_gmlp_body, aw=aw, groups=groups, n_chunks=tm // CHUNK)
    return pl.pallas_call(
        body,
        grid=(m // tm,),
        in_specs=[
            pl.BlockSpec((tm, d), lambda i: (i, 0)),
            pl.BlockSpec((1, d), lambda i: (0, 0)),
            mod_spec, mod_spec,
            pl.BlockSpec((d, two_aw), lambda i: (0, 0), pipeline_mode=pl.Buffered(1)),
            pl.BlockSpec((1, aw), lambda i: (0, 0)),
            pl.BlockSpec((groups, CHUNK, CHUNK), lambda i: (0, 0, 0)),
            pl.BlockSpec((CHUNK, aw), lambda i: (0, 0)),
        ],
        out_specs=pl.BlockSpec((tm, aw), lambda i: (i, 0)),
        out_shape=jax.ShapeDtypeStruct((m, aw), BF16),
        scratch_shapes=[pltpu.VMEM((tm, d), BF16), pltpu.VMEM((tm, two_aw), F32)],
        compiler_params=_cparams(("arbitrary",)),
    )(x, g.reshape(1, d), shift, scale, w_a, gain.reshape(1, aw), ws, b_full)


def _prep_body(x_ref, xp_ref, xn_ref, g_ref, sh_ref, sc_ref, w_ref, mu_ref, w0_ref, w2_ref, a0_ref, a2_ref,
               g2_ref, kk_ref, ka_ref, rk_ref, e_ref, r_o, v_o, kk_o, g_o, bonus_o, lw_o, kka_o, kd_o,
               h_ref, p_ref, *, bw, wl2, al2, seq, tm):
    i = pl.program_id(0)
    g, sh, sc = g_ref[...], sh_ref[...], sc_ref[...]
    _prenorm_rows(x_ref, g, sh, sc, h_ref)
    _prenorm_rows(xp_ref, g, sh, sc, h_ref, dst_row=tm)
    _prenorm_rows(xn_ref, g, sh, sc, h_ref, dst_row=tm + ROW_BLOCK)
    p_ref[...] = _dot(h_ref[...], w_ref[...])

    p = p_ref[0:tm, :]
    row = lax.broadcasted_iota(jnp.int32, p.shape, 0)
    first = (i * tm) % seq == 0
    last = ((i + 1) * tm) % seq == 0
    prev_row = jnp.where(first, 0.0, p_ref[tm + ROW_BLOCK - 1:tm + ROW_BLOCK, :])
    next_row = jnp.where(last, 0.0, p_ref[tm + ROW_BLOCK:tm + ROW_BLOCK + 1, :])
    prev = jnp.where(row == 0, prev_row, pltpu.roll(p, 1, axis=0))
    nxt = jnp.where(row == tm - 1, next_row, pltpu.roll(p, tm - 1, axis=0))
    mu = mu_ref[...]
    p = p + mu[0:1] * (prev - p) + mu[1:2] * (nxt - p)

    r = p[:, 0:bw]
    k = p[:, bw:2 * bw]
    v = p[:, 2 * bw:3 * bw]
    o = 3 * bw
    wd = jnp.tanh(p[:, o:o + wl2]).astype(BF16)
    ad = p[:, o + wl2:o + wl2 + al2].astype(BF16)
    gdn = jax.nn.sigmoid(p[:, o + wl2 + al2:]).astype(BF16)

    lw = -W_DECAY_SCALE * jax.nn.sigmoid(w0_ref[...] + _dot(wd, w2_ref[...]))
    a = jax.nn.sigmoid(a0_ref[...] + _dot(ad, a2_ref[...]))
    gate = _dot(gdn, g2_ref[...])

    e = e_ref[...]
    kq = k * kk_ref[...]
    kk = kq * lax.rsqrt(_head_sums(kq * kq, e) + KK_EPS)
    ka = ka_ref[...]
    kd0 = k * (1.0 + (a[:, 0:bw] - 1.0) * ka)
    kd1 = k * (1.0 + (a[:, bw:] - 1.0) * ka)
    kb = 0.5 * (kd0 + kd1)
    bonus = _head_sums(r * kb * rk_ref[...], e) * v

    r_o[...] = r
    v_o[...] = v
    kk_o[...] = kk
    g_o[...] = gate
    bonus_o[...] = bonus
    lw_o[...] = lw
    kka_o[:, 0:bw] = kk * a[:, 0:bw]
    kka_o[:, bw:] = kk * a[:, bw:]
    kd_o[:, 0:bw] = kd0
    kd_o[:, bw:] = kd1


def _rwkv_front(x, g, shift, scale, w_b, seq, consts, *, rows_per_mod, tm=256):
    m, d = x.shape
    pw = w_b.shape[1]
    (mu, w0, w2cat, a0, a2cat, g2p, k_k, k_a, r_k, e) = consts
    bw = k_k.shape[1]
    wl2, al2 = w2cat.shape[0], a2cat.shape[0]
    tm = min(tm, seq)
    tpb = rows_per_mod // tm
    nblk = tm // ROW_BLOCK
    last_blk = m // ROW_BLOCK - 1
    full = lambda arr: pl.BlockSpec(arr.shape, lambda i: (0,) * arr.ndim)
    mod_spec = pl.BlockSpec((None, 1, d), lambda i: (i // tpb, 0, 0))
    body = functools.partial(_prep_body, bw=bw, wl2=wl2, al2=al2, seq=seq, tm=tm)
    one = jax.ShapeDtypeStruct((m, bw), F32)
    two = jax.ShapeDtypeStruct((m, 2 * bw), F32)
    o1 = pl.BlockSpec((tm, bw), lambda i: (i, 0))
    o2 = pl.BlockSpec((tm, 2 * bw), lambda i: (i, 0))
    return pl.pallas_call(
        body,
        grid=(m // tm,),
        in_specs=[
            pl.BlockSpec((tm, d), lambda i: (i, 0)),
            pl.BlockSpec((ROW_BLOCK, d), lambda i: (jnp.maximum(i * nblk - 1, 0), 0)),
            pl.BlockSpec((ROW_BLOCK, d), lambda i: (jnp.minimum((i + 1) * nblk, last_blk), 0)),
            pl.BlockSpec((1, d), lambda i: (0, 0)),
            mod_spec, mod_spec,
            pl.BlockSpec((d, pw), lambda i: (0, 0), pipeline_mode=pl.Buffered(1)),
            full(mu), full(w0), full(w2cat), full(a0), full(a2cat), full(g2p), full(k_k), full(k_a),
            full(r_k), full(e),
        ],
        out_specs=[o1, o1, o1, o1, o1, o2, o2, o2],
        out_shape=[one, one, one, one, one, two, two, two],
        scratch_shapes=[pltpu.VMEM((tm + 2 * ROW_BLOCK, d), BF16), pltpu.VMEM((tm + 2 * ROW_BLOCK, pw), F32)],
        compiler_params=_cparams(("arbitrary",)),
    )(x, x, x, g.reshape(1, d), shift, scale, w_b, mu, w0, w2cat, a0, a2cat, g2p, k_k, k_a, r_k, e)


SCAN_CHUNK = 64
INV_BLOCK = 16


def _dot_nt(a, b):
    return lax.dot_general(a, b, (((1,), (1,)), ((), ())), preferred_element_type=F32)


def _dot_tn(a, b):
    return lax.dot_general(a, b, (((0,), (0,)), ((), ())), preferred_element_type=F32)


def _pair_blockdiag(x, lane_lo):
    return jnp.concatenate([jnp.where(lane_lo, x, 0.0), jnp.where(lane_lo, 0.0, x)], axis=0).astype(BF16)


def _chunk_scan_body(r_ref, v_ref, kk_ref, lw_ref, kka_ref, kd_ref, s0_ref, y_ref, sout_ref,
                     s_ref, qw_ref, u0_ref, arb_ref, pe_ref, vb_ref, elt_ref, *, reverse, nsub, pairs, group):
    c = SCAN_CHUNK
    step = pl.program_id(1)
    bf = lambda t: t.astype(BF16)

    @pl.when(step == 0)
    def _():
        s_ref[...] = s0_ref[...]

    ri = lax.broadcasted_iota(jnp.int32, (c, c), 0)
    ci = lax.broadcasted_iota(jnp.int32, (c, c), 1)
    tri = ((ri <= ci) if reverse else (ri >= ci)).astype(BF16)
    rp = lax.broadcasted_iota(jnp.int32, (c, LANE), 0)
    cp = lax.broadcasted_iota(jnp.int32, (c, LANE), 1) % c
    incl_p = (rp <= cp) if reverse else (rp >= cp)
    strict_p = (rp < cp) if reverse else (rp > cp)
    blk_p = (rp // INV_BLOCK) == (cp // INV_BLOCK)
    eye_p = (rp == cp).astype(F32)
    lane_lo = lax.broadcasted_iota(jnp.int32, (c, LANE), 1) < HEAD_DIM
    bi = lax.broadcasted_iota(jnp.int32, (LANE, LANE), 0) // HEAD_DIM
    bj = lax.broadcasted_iota(jnp.int32, (LANE, LANE), 1) // HEAD_DIM
    same_head = bi == bj
    bd = lambda t: _pair_blockdiag(t, lane_lo)
    prange = range(pairs)
    lanes = [pl.ds(p * LANE, LANE) for p in prange]

    def phase_a(jg, carry):
        streams = [(jg * group + gi, p) for gi in range(group) for p in prange]
        sr = range(len(streams))
        rows = [pl.ds(pl.multiple_of(j * c, c), c) for j, _ in streams]
        cols = [lanes[p] for _, p in streams]
        lw = [lw_ref[rows[s], cols[s]] for s in sr]
        lc = []
        for s in sr:
            hi = bf(lw[s])
            lc.append(_dot(tri, hi) + _dot(tri, bf(lw[s] - hi.astype(F32))))
        qq, qk_bd, qr, pk_bd, pb_bd, v_bd = [], [], [], [], [], []
        for s, (j, p) in enumerate(streams):
            ltot = lc[s][0:1] if reverse else lc[s][c - 1:c]
            e_neg = jnp.exp(-lc[s])
            e_end = jnp.exp(ltot - lc[s])
            kd = kd_ref[rows[s], cols[s]]
            kka = kka_ref[rows[s], cols[s]]
            v = v_ref[rows[s], cols[s]]
            qk = kk_ref[rows[s], cols[s]] * jnp.exp(lc[s] - lw[s])
            qr.append(r_ref[rows[s], cols[s]] * jnp.exp(lc[s]))
            qq.append(bf(jnp.concatenate([qk, qr[s]], axis=0)))
            qk_bd.append(bd(qk))
            pk_bd.append(bd(kd * e_neg))
            pb_bd.append(bd(kka * e_neg))
            v_bd.append(bd(v))
            pe_ref[j, p] = bf(jnp.concatenate([kd * e_end, -(kka * e_end)], axis=0))
            vb_ref[j, p] = bf(v)
            elt_ref[j, p] = jnp.broadcast_to(jnp.exp(ltot), (SUBLANE, LANE))
        g1 = [_dot_nt(qq[s], pk_bd[s]) for s in sr]
        g2 = [_dot_nt(qq[s], pb_bd[s]) for s in sr]
        akk = [bf(jnp.where(strict_p, g1[s][:c], 0.0)) for s in sr]
        ark = [bf(jnp.where(incl_p, g1[s][c:], 0.0)) for s in sr]
        nmat = [jnp.where(strict_p, g2[s][:c], 0.0) for s in sr]
        for s, (j, p) in enumerate(streams):
            arb_ref[j, p] = bf(jnp.where(incl_p, g2[s][c:], 0.0))
        av = [_dot(akk[s], v_bd[s]) for s in sr]
        for s in sr:
            y_ref[rows[s], cols[s]] = _dot(ark[s], v_bd[s])
        nd = [jnp.where(blk_p, nmat[s], 0.0) for s in sr]
        lo_bd = [bd(nmat[s] - nd[s]) for s in sr]
        mpow = [-nd[s] for s in sr]
        dinv = [eye_p + mpow[s] for s in sr]
        for _ in range(int(math.log2(INV_BLOCK)) - 1):
            mpow = [_dot(bf(mpow[s]), bd(mpow[s])) for s in sr]
            dinv = [dinv[s] + _dot(bf(dinv[s]), bd(mpow[s])) for s in sr]
        dinv_b = [bf(dinv[s]) for s in sr]
        x1 = [_dot(dinv_b[s], lo_bd[s]) for s in sr]
        x1b = [bf(x1[s]) for s in sr]
        acc = [eye_p - x1[s] for s in sr]
        xp = x1
        for k in range(2, SCAN_CHUNK // INV_BLOCK):
            xp = [_dot(x1b[s], bd(xp[s])) for s in sr]
            acc = [acc[s] + xp[s] if k % 2 == 0 else acc[s] - xp[s] for s in sr]
        tinv = [bf(_dot(bf(acc[s]), bd(dinv[s]))) for s in sr]
        wt = [_dot(tinv[s], qk_bd[s]) for s in sr]
        for s, (j, p) in enumerate(streams):
            u0_ref[j, p] = _dot(tinv[s], bd(av[s]))
            qw_ref[j, p] = bf(jnp.concatenate([wt[s], qr[s]], axis=0))
        return carry

    lax.fori_loop(0, nsub // group, phase_a, 0)

    def phase_b(j, carry):
        jj = (nsub - 1 - j) if reverse else j
        rows = pl.ds(pl.multiple_of(jj * c, c), c)
        s = [s_ref[p] for p in prange]
        xs = [_dot_nt(qw_ref[jj, p], bf(s[p])) for p in prange]
        u = [u0_ref[jj, p] + xs[p][:c] for p in prange]
        au = [_dot(arb_ref[jj, p], bd(u[p])) for p in prange]
        for p in prange:
            y_ref[rows, lanes[p]] = y_ref[rows, lanes[p]] + xs[p][c:] - au[p]
        upd = [_dot_tn(jnp.concatenate([vb_ref[jj, p], bf(u[p])], axis=0), pe_ref[jj, p]) for p in prange]
        for p in prange:
            s_ref[p] = s[p] * elt_ref[jj, p][0:1] + jnp.where(same_head, upd[p], 0.0)
        return carry

    lax.fori_loop(0, nsub, phase_b, 0)

    @pl.when(step == pl.num_programs(1) - 1)
    def _():
        sout_ref[...] = s_ref[...]


def _chunk_scan(r, v, kk, lw, kka, kd, s0, *, bn, reverse, tt=256):
    m, bw = r.shape
    seq = m // bn
    tt = min(tt, seq)
    steps = seq // tt
    pairs = bw // LANE
    d = 1 if reverse else 0

    def row_blk(b, s):
        return b * steps + (steps - 1 - s if reverse else s)

    tok = pl.BlockSpec((tt, bw), lambda b, s: (row_blk(b, s), 0))
    tok_dir = pl.BlockSpec((tt, bw), lambda b, s: (row_blk(b, s), d))
    st = pl.BlockSpec((None, pairs, LANE, LANE), lambda b, s: (b, 0, 0, 0))
    c = SCAN_CHUNK
    nsub = tt // c
    group = 2 if nsub % 2 == 0 else 1
    body = functools.partial(_chunk_scan_body, reverse=reverse, nsub=nsub, pairs=pairs, group=group)
    return pl.pallas_call(
        body,
        grid=(bn, steps),
        in_specs=[tok, tok, tok, tok_dir, tok_dir, tok_dir, st],
        out_specs=[tok, st],
        out_shape=[jax.ShapeDtypeStruct((m, bw), F32), jax.ShapeDtypeStruct(s0.shape, F32)],
        scratch_shapes=[
            pltpu.VMEM((pairs, LANE, LANE), F32),
            pltpu.VMEM((nsub, pairs, 2 * c, LANE), BF16),
            pltpu.VMEM((nsub, pairs, c, LANE), F32),
            pltpu.VMEM((nsub, pairs, c, LANE), BF16),
            pltpu.VMEM((nsub, pairs, 2 * c, LANE), BF16),
            pltpu.VMEM((nsub, pairs, c, LANE), BF16),
            pltpu.VMEM((nsub, pairs, SUBLANE, LANE), F32),
        ],
        compiler_params=_cparams(("arbitrary", "arbitrary")),
    )(r, v, kk, lw, kka, kd, s0)


def _mixout_body(x_ref, gt_ref, ya_ref, yf_ref, yb_ref, bonus_ref, g_ref, lng_ref, lnb_ref, e_ref, wa_ref,
                 wb_ref, o_ref):
    e = e_ref[...]
    inv_n = 1.0 / HEAD_DIM
    y = yf_ref[...] + yb_ref[...]
    yc = y - _head_sums(y, e) * inv_n
    var = _head_sums(yc * yc, e) * inv_n
    yn = yc * lax.rsqrt(var + GN_EPS) * lng_ref[...] + lnb_ref[...]
    yb = ((yn + bonus_ref[...]) * g_ref[...]).astype(BF16)
    ox = _dot(ya_ref[...], wa_ref[...]) + _dot(yb, wb_ref[...])
    o_ref[...] = x_ref[...] + gt_ref[...] * ox


def _mix_out(x, gate, ya, y_fwd, y_bwd, bonus, g, ln_g, ln_b, e, w_out_a, w_out_b, *, rows_per_mod, tm=256):
    m, d = x.shape
    aw, bw = ya.shape[1], y_fwd.shape[1]
    tpb = rows_per_mod // tm
    full = lambda arr: pl.BlockSpec(arr.shape, lambda i: (0,) * arr.ndim)
    tok = lambda width: pl.BlockSpec((tm, width), lambda i: (i, 0))
    return pl.pallas_call(
        _mixout_body,
        grid=(m // tm,),
        in_specs=[
            tok(d), pl.BlockSpec((None, 1, d), lambda i: (i // tpb, 0, 0)),
            tok(aw), tok(bw), tok(bw), tok(bw), tok(bw),
            full(ln_g), full(ln_b), full(e), full(w_out_a), full(w_out_b),
        ],
        out_specs=tok(d),
        out_shape=jax.ShapeDtypeStruct((m, d), F32),
        compiler_params=_cparams(("arbitrary",)),
    )(x, gate, ya, y_fwd, y_bwd, bonus, g, ln_g, ln_b, e, w_out_a, w_out_b)


def kernel(x, c, ctx, c_ctx, ada_w, ada_b, norm_g, ffn_w1, ffn_w3, ffn_w2, ab_w_in, ab_w_out, gm_v_gain,
           gm_ws, gm_b, rw_mu, rw_w0, rw_w2, rw_a0, rw_a2, rw_g2, rw_k_k, rw_k_a, rw_r_k, rw_ln_g, rw_ln_b,
           sc_w_in, sc_conv, sc_w_out, final_g):
    bn, seq, d = x.shape
    ctx_len = ctx.shape[1]
    depth = ada_w.shape[0]
    n_mod = ada_w.shape[2] // d
    m_x, m_c = bn * seq, bn * ctx_len
    xs = x.reshape(m_x, d)
    cs = ctx.reshape(m_c, d)

    rows = _round_up(bn + 1, SUBLANE)
    c_rows = jnp.zeros((rows, d), F32).at[:bn].set(c).at[bn].set(c_ctx)
    mods = _modulation(c_rows, ada_w, ada_b).reshape(depth, rows, n_mod, d)

    for i in range(depth):
        even = i % 2 == 0
        ctx_after = any(j % 2 == 0 for j in range(i + 1, depth))
        use_ctx = cs is not None and (even or ctx_after)
        if not use_ctx:
            cs = None
        mx = [mods[i, :bn, j].reshape(bn, 1, d) for j in range(n_mod)]
        mc = [mods[i, bn:bn + 1, j].reshape(1, 1, d) for j in range(n_mod)]
        w1 = ffn_w1[i].astype(BF16)
        w3 = ffn_w3[i].astype(BF16)
        w2 = ffn_w2[i].astype(BF16)
        last_layer = i == depth - 1

        xs = _glu_block(xs, norm_g[i, 0], mx[0], mx[1], mx[2], [w1[0], w3[0]], w2[0], kind="swiglu",
                        coef=0.5, rows_per_mod=seq)
        if cs is not None:
            cs = _glu_block(cs, norm_g[i, 0], mc[0], mc[1], mc[2], [w1[0], w3[0]], w2[0], kind="swiglu",
                            coef=0.5, rows_per_mod=m_c)

        if even:
            e_idx = i // 2
            if ctx_after:
                raise NotImplementedError("context output of an even layer is only needed for depth > 2")
            xs = _ab_mixer(xs, cs, mx, mc, norm_g[i, 1], bn, seq, ctx_len, ab_w_in[e_idx], ab_w_out[e_idx],
                           gm_v_gain[e_idx], gm_ws[e_idx], gm_b[e_idx], rw_mu[e_idx], rw_w0[e_idx],
                           rw_w2[e_idx], rw_a0[e_idx], rw_a2[e_idx], rw_g2[e_idx], rw_k_k[e_idx],
                           rw_k_a[e_idx], rw_r_k[e_idx], rw_ln_g[e_idx], rw_ln_b[e_idx])
        else:
            o_idx = i // 2
            if ctx_after:
                raise NotImplementedError("context output of an odd layer is only needed for depth > 2")
            w_in = sc_w_in[o_idx].astype(BF16)
            ups = [w_in[:, 0:d], w_in[:, d:2 * d], w_in[:, 2 * d:3 * d]]
            xs = _glu_block(xs, norm_g[i, 1], mx[3], mx[4], mx[5], ups, sc_w_out[o_idx].astype(BF16),
                            kind="conv", coef=1.0, conv_w=sc_conv[o_idx], rows_per_mod=seq, tm=512)
        cs = None

        xs = _glu_block(xs, norm_g[i, 2], mx[6], mx[7], mx[8], [w1[1], w3[1]], w2[1], kind="swiglu",
                        coef=0.5, final_g=final_g if last_layer else None, rows_per_mod=seq)
    return xs.reshape(bn, seq, d)


def _ab_mixer(xs, cs, mx, mc, g, bn, seq, ctx_len, w_in, w_out, v_gain, ws, b_s, mu, w0, w2, a0, a2, g2,
              k_k, k_a, r_k, ln_g, ln_b):
    m_x, d = xs.shape
    m_c = cs.shape[0]
    bw = k_k.shape[0]
    aw = v_gain.shape[0]
    a_cols = 2 * aw
    heads = bw // HEAD_DIM
    wl, al, gl = w2.shape[1], a2.shape[1], g2.shape[0]
    b_cols = w_in.shape[1] - a_cols
    pw = _round_up(b_cols, 512)
    gp = pw - (3 * bw + 2 * wl + 2 * al)

    w_in_bf = w_in.astype(BF16)
    w_in_a = w_in_bf[:, :a_cols]
    w_in_b = jnp.pad(w_in_bf[:, a_cols:], ((0, 0), (0, pw - b_cols)))

    mu_p = jnp.pad(mu, ((0, 0), (0, pw - b_cols)))
    zeros_w = jnp.zeros((wl, bw), F32)
    zeros_a = jnp.zeros((al, bw), F32)
    w2cat = jnp.concatenate([jnp.concatenate([w2[0], zeros_w], 1), jnp.concatenate([zeros_w, w2[1]], 1)], 0)
    a2cat = jnp.concatenate([jnp.concatenate([a2[0], zeros_a], 1), jnp.concatenate([zeros_a, a2[1]], 1)], 0)
    g2p = jnp.pad(g2, ((0, gp - gl), (0, 0)))
    head_id = jnp.arange(min(bw, MXU_WIDTH)) // HEAD_DIM
    e = (head_id[:, None] == head_id[None, :]).astype(BF16)
    consts = (mu_p, w0.reshape(1, 2 * bw), w2cat.astype(BF16), a0.reshape(1, 2 * bw), a2cat.astype(BF16),
              g2p.astype(BF16), k_k.reshape(1, bw), k_a.reshape(1, bw), r_k.reshape(1, bw), e)

    b_full = jnp.repeat(b_s.T, aw // ws.shape[0], axis=1)
    ya = _gmlp_branch(xs, g, mx[3], mx[4], w_in_a, v_gain, ws.astype(BF16), b_full, rows_per_mod=seq)

    fx = _rwkv_front(xs, g, mx[3], mx[4], w_in_b, seq, consts, rows_per_mod=seq)
    fc = _rwkv_front(cs, g, mc[3], mc[4], w_in_b, ctx_len, consts, rows_per_mod=m_c)
    r_x, v_x, kk_x, g_x, bonus_x, lw_x, kka_x, kd_x = fx
    r_c, v_c, kk_c, _, _, lw_c, kka_c, kd_c = fc
    s_zero = jnp.zeros((bn, bw // LANE, LANE, LANE), F32)
    ys = []
    for reverse in (False, True):
        _, s_ctx = _chunk_scan(r_c, v_c, kk_c, lw_c, kka_c, kd_c, s_zero, bn=bn, reverse=reverse)
        y_dir, _ = _chunk_scan(r_x, v_x, kk_x, lw_x, kka_x, kd_x, s_ctx, bn=bn, reverse=reverse)
        ys.append(y_dir)

    w_out_bf = w_out.astype(BF16)
    return _mix_out(xs, mx[5], ya, ys[0], ys[1], bonus_x, g_x, ln_g.reshape(1, bw), ln_b.reshape(1, bw), e,
                    w_out_bf[:aw], w_out_bf[aw:], rows_per_mod=seq)
```

```python
import functools
import math

import jax
import jax.numpy as jnp
from jax import lax
from jax.experimental import pallas as pl
from jax.experimental.pallas import tpu as pltpu

F32 = jnp.float32
BF16 = jnp.bfloat16

GRID_W = 64
CHUNK = 128
HEAD_DIM = 64
RMS_EPS = 1e-6
LN_EPS = 1e-5
GN_EPS = 64e-5
KK_EPS = 1e-12
W_DECAY_SCALE = math.exp(-0.5)

LANE = 128
SUBLANE = 8
MXU_WIDTH = 256
VMEM_LIMIT = 56 * 1024 * 1024


def _cparams(sem):
    return pltpu.CompilerParams(dimension_semantics=sem, vmem_limit_bytes=VMEM_LIMIT)


def _round_up(n, m):
    return (n + m - 1) // m * m


def _dot(a, b):
    return jnp.dot(a, b, preferred_element_type=F32)


def _dot2(a, b):
    hi = a.astype(BF16)
    lo = (a - hi.astype(F32)).astype(BF16)
    return _dot(hi, b) + _dot(lo, b)


def _head_sums(x, e):
    blk = e.shape[0]
    parts = [_dot2(x[:, j:j + blk], e) for j in range(0, x.shape[1], blk)]
    return parts[0] if len(parts) == 1 else jnp.concatenate(parts, axis=1)


ROW_BLOCK = 16
ROW_UNROLL = 8


def _prenorm_rows(x_ref, g, shift, scale, h_ref, *, src_row=0, dst_row=0, n_rows=None):
    n_rows = x_ref.shape[0] if n_rows is None else n_rows
    gs = g * (1.0 + scale)

    def block(i, carry):
        off = pl.multiple_of(i * ROW_BLOCK, ROW_BLOCK)
        x = x_ref[pl.ds(src_row + off, ROW_BLOCK), :]
        ms = jnp.mean(x * x, axis=-1, keepdims=True)
        h_ref[pl.ds(dst_row + off, ROW_BLOCK), :] = (x * lax.rsqrt(ms + RMS_EPS) * gs + shift).astype(h_ref.dtype)
        return carry

    n_blocks = n_rows // ROW_BLOCK
    lax.fori_loop(0, n_blocks, block, 0, unroll=math.gcd(n_blocks, ROW_UNROLL))


def _mod_body(c_ref, w_ref, b_ref, o_ref):
    s = jax.nn.silu(c_ref[...]).astype(BF16)
    o_ref[...] = _dot(s, w_ref[...].astype(BF16)) + b_ref[...]


def _modulation(c_rows, ada_w, ada_b):
    depth, d, n = ada_w.shape
    tn = 1024 if n % 1024 == 0 else 512
    rows = c_rows.shape[0]
    return pl.pallas_call(
        _mod_body,
        grid=(depth, n // tn),
        in_specs=[
            pl.BlockSpec((rows, d), lambda l, j: (0, 0)),
            pl.BlockSpec((None, d, tn), lambda l, j: (l, 0, j)),
            pl.BlockSpec((None, 1, tn), lambda l, j: (l, 0, j)),
        ],
        out_specs=pl.BlockSpec((None, rows, tn), lambda l, j: (l, 0, j)),
        out_shape=jax.ShapeDtypeStruct((depth, rows, n), F32),
        compiler_params=_cparams(("arbitrary", "arbitrary")),
    )(c_rows, ada_w, ada_b.reshape(depth, 1, n))


def _glu_body(*refs, n_up, kind, coef, final, nf):
    x_ref, g_ref, sh_ref, sc_ref, gt_ref = refs[:5]
    up_refs = refs[5:5 + n_up]
    down_ref = refs[5 + n_up]
    idx = 6 + n_up
    cw_ref = fg_ref = None
    if kind == "conv":
        cw_ref = refs[idx]
        idx += 1
    if final:
        fg_ref = refs[idx]
        idx += 1
    o_ref, h_ref = refs[idx], refs[idx + 1]
    f = pl.program_id(1)

    @pl.when(f == 0)
    def _():
        _prenorm_rows(x_ref, g_ref[...], sh_ref[...], sc_ref[...], h_ref)
        o_ref[...] = jnp.zeros_like(o_ref)

    h = h_ref[...]
    ups = [_dot(h, u[...]) for u in up_refs]
    if kind == "swiglu":
        mid = jax.nn.silu(ups[0]) * ups[1]
    else:
        z = ups[1] * ups[2]
        rows = z.shape[0]
        col = lax.broadcasted_iota(jnp.int32, z.shape, 0) % GRID_W
        zp = jnp.where(col == 0, 0.0, pltpu.roll(z, 1, axis=0))
        zn = jnp.where(col == GRID_W - 1, 0.0, pltpu.roll(z, rows - 1, axis=0))
        cw = cw_ref[...]
        mid = ups[0] * (cw[0:1] * zp + cw[1:2] * z + cw[2:3] * zn)
    o_ref[...] += _dot(mid.astype(BF16), down_ref[...])

    @pl.when(f == nf - 1)
    def _():
        res = x_ref[...] + (coef * gt_ref[...]) * o_ref[...]
        if final:
            ms = jnp.mean(res * res, axis=-1, keepdims=True)
            res = res * lax.rsqrt(ms + RMS_EPS) * fg_ref[...]
        o_ref[...] = res


def _glu_block(x, g, shift, scale, gate, ups, down, *, kind, coef, conv_w=None, final_g=None,
               rows_per_mod, tm=1024, tf=512):
    m, d = x.shape
    fdim = down.shape[0]
    tm = min(tm, m, rows_per_mod)
    nf = fdim // tf
    tpb = rows_per_mod // tm
    mod_spec = pl.BlockSpec((None, 1, d), lambda i, f: (i // tpb, 0, 0))
    x_mode = dict(pipeline_mode=pl.Buffered(1)) if (tm > 512 and final_g is not None) else {}
    in_specs = [
        pl.BlockSpec((tm, d), lambda i, f: (i, 0), **x_mode),
        pl.BlockSpec((1, d), lambda i, f: (0, 0)),
        mod_spec, mod_spec, mod_spec,
    ]
    args = [x, g.reshape(1, d), shift, scale, gate]
    for u in ups:
        in_specs.append(pl.BlockSpec((d, tf), lambda i, f: (0, f)))
        args.append(u)
    in_specs.append(pl.BlockSpec((tf, d), lambda i, f: (f, 0)))
    args.append(down)
    if kind == "conv":
        in_specs.append(pl.BlockSpec((3, tf), lambda i, f: (0, f)))
        args.append(conv_w)
    if final_g is not None:
        in_specs.append(pl.BlockSpec((1, d), lambda i, f: (0, 0)))
        args.append(final_g.reshape(1, d))
    body = functools.partial(_glu_body, n_up=len(ups), kind=kind, coef=coef,
                             final=final_g is not None, nf=nf)
    return pl.pallas_call(
        body,
        grid=(m // tm, nf),
        in_specs=in_specs,
        out_specs=pl.BlockSpec((tm, d), lambda i, f: (i, 0)),
        out_shape=jax.ShapeDtypeStruct((m, d), F32),
        scratch_shapes=[pltpu.VMEM((tm, d), BF16)],
        compiler_params=_cparams(("arbitrary", "arbitrary")),
    )(*args)


def _gmlp_body(x_ref, g_ref, sh_ref, sc_ref, w_ref, gain_ref, ws_ref, b_ref, o_ref, h_ref, p_ref, *,
               aw, groups, n_chunks):
    _prenorm_rows(x_ref, g_ref[...], sh_ref[...], sc_ref[...], h_ref)
    p_ref[...] = _dot(h_ref[...], w_ref[...])
    gd = aw // groups
    for c in range(n_chunks):
        rows = pl.ds(c * CHUNK, CHUNK)
        u = jax.nn.gelu(p_ref[rows, 0:aw])
        v = jax.nn.gelu(p_ref[rows, aw:2 * aw])
        vm = jnp.mean(v, axis=-1, keepdims=True)
        vc = v - vm
        vv = jnp.mean(vc * vc, axis=-1, keepdims=True)
        vn = (vc * lax.rsqrt(vv + LN_EPS) * gain_ref[...]).astype(BF16)
        for gi in range(groups):
            cols = slice(gi * gd, (gi + 1) * gd)
            vs = _dot(ws_ref[gi], vn[:, cols]) + b_ref[:, cols]
            o_ref[rows, cols] = (u[:, cols] * vs).astype(o_ref.dtype)


def _gmlp_branch(x, g, shift, scale, w_a, gain, ws, b_full, *, rows_per_mod, tm=512):
    m, d = x.shape
    two_aw = w_a.shape[1]
    aw = two_aw // 2
    groups = ws.shape[0]
    tm = min(tm, m, rows_per_mod)
    tpb = rows_per_mod // tm
    mod_spec = pl.BlockSpec((None, 1, d), lambda i: (i // tpb, 0, 0))
    body = functools.partial(_gmlp_body, aw=aw, groups=groups, n_chunks=tm // CHUNK)
    return pl.pallas_call(
        body,
        grid=(m // tm,),
        in_specs=[
            pl.BlockSpec((tm, d), lambda i: (i, 0)),
            pl.BlockSpec((1, d), lambda i: (0, 0)),
            mod_spec, mod_spec,
            pl.BlockSpec((d, two_aw), lambda i: (0, 0), pipeline_mode=pl.Buffered(1)),
            pl.BlockSpec((1, aw), lambda i: (0, 0)),
            pl.BlockSpec((groups, CHUNK, CHUNK), lambda i: (0, 0, 0)),
            pl.BlockSpec((CHUNK, aw), lambda i: (0, 0)),
        ],
        out_specs=pl.BlockSpec((tm, aw), lambda i: (i, 0)),
        out_shape=jax.ShapeDtypeStruct((m, aw), BF16),
        scratch_shapes=[pltpu.VMEM((tm, d), BF16), pltpu.VMEM((tm, two_aw), F32)],
        compiler_params=_cparams(("arbitrary",)),
    )(x, g.reshape(1, d), shift, scale, w_a, gain.reshape(1, aw), ws, b_full)


def _prep_body(x_ref, xp_ref, xn_ref, g_ref, sh_ref, sc_ref, w_ref, mu_ref, w0_ref, w2_ref, a0_ref, a2_ref,
               g2_ref, kk_ref, ka_ref, rk_ref, e_ref, r_o, v_o, kk_o, g_o, bonus_o, lw_o, kka_o, kd_o,
               h_ref, p_ref, *, bw, wl2, al2, seq, tm):
    i = pl.program_id(0)
    g, sh, sc = g_ref[...], sh_ref[...], sc_ref[...]
    _prenorm_rows(x_ref, g, sh, sc, h_ref)
    _prenorm_rows(xp_ref, g, sh, sc, h_ref, dst_row=tm)
    _prenorm_rows(xn_ref, g, sh, sc, h_ref, dst_row=tm + ROW_BLOCK)
    p_ref[...] = _dot(h_ref[...], w_ref[...])

    p = p_ref[0:tm, :]
    row = lax.broadcasted_iota(jnp.int32, p.shape, 0)
    first = (i * tm) % seq == 0
    last = ((i + 1) * tm) % seq == 0
    prev_row = jnp.where(first, 0.0, p_ref[tm + ROW_BLOCK - 1:tm + ROW_BLOCK, :])
    next_row = jnp.where(last, 0.0, p_ref[tm + ROW_BLOCK:tm + ROW_BLOCK + 1, :])
    prev = jnp.where(row == 0, prev_row, pltpu.roll(p, 1, axis=0))
    nxt = jnp.where(row == tm - 1, next_row, pltpu.roll(p, tm - 1, axis=0))
    mu = mu_ref[...]
    p = p + mu[0:1] * (prev - p) + mu[1:2] * (nxt - p)

    r = p[:, 0:bw]
    k = p[:, bw:2 * bw]
    v = p[:, 2 * bw:3 * bw]
    o = 3 * bw
    wd = jnp.tanh(p[:, o:o + wl2]).astype(BF16)
    ad = p[:, o + wl2:o + wl2 + al2].astype(BF16)
    gdn = jax.nn.sigmoid(p[:, o + wl2 + al2:]).astype(BF16)

    lw = -W_DECAY_SCALE * jax.nn.sigmoid(w0_ref[...] + _dot(wd, w2_ref[...]))
    a = jax.nn.sigmoid(a0_ref[...] + _dot(ad, a2_ref[...]))
    gate = _dot(gdn, g2_ref[...])

    e = e_ref[...]
    kq = k * kk_ref[...]
    kk = kq * lax.rsqrt(_head_sums(kq * kq, e) + KK_EPS)
    ka = ka_ref[...]
    kd0 = k * (1.0 + (a[:, 0:bw] - 1.0) * ka)
    kd1 = k * (1.0 + (a[:, bw:] - 1.0) * ka)
    kb = 0.5 * (kd0 + kd1)
    bonus = _head_sums(r * kb * rk_ref[...], e) * v

    r_o[...] = r
    v_o[...] = v
    kk_o[...] = kk
    g_o[...] = gate
    bonus_o[...] = bonus
    lw_o[...] = lw
    kka_o[:, 0:bw] = kk * a[:, 0:bw]
    kka_o[:, bw:] = kk * a[:, bw:]
    kd_o[:, 0:bw] = kd0
    kd_o[:, bw:] = kd1


def _rwkv_front(x, g, shift, scale, w_b, seq, consts, *, rows_per_mod, tm=256):
    m, d = x.shape
    pw = w_b.shape[1]
    (mu, w0, w2cat, a0, a2cat, g2p, k_k, k_a, r_k, e) = consts
    bw = k_k.shape[1]
    wl2, al2 = w2cat.shape[0], a2cat.shape[0]
    tm = min(tm, seq)
    tpb = rows_per_mod // tm
    nblk = tm // ROW_BLOCK
    last_blk = m // ROW_BLOCK - 1
    full = lambda arr: pl.BlockSpec(arr.shape, lambda i: (0,) * arr.ndim)
    mod_spec = pl.BlockSpec((None, 1, d), lambda i: (i // tpb, 0, 0))
    body = functools.partial(_prep_body, bw=bw, wl2=wl2, al2=al2, seq=seq, tm=tm)
    one = jax.ShapeDtypeStruct((m, bw), F32)
    two = jax.ShapeDtypeStruct((m, 2 * bw), F32)
    o1 = pl.BlockSpec((tm, bw), lambda i: (i, 0))
    o2 = pl.BlockSpec((tm, 2 * bw), lambda i: (i, 0))
    return pl.pallas_call(
        body,
        grid=(m // tm,),
        in_specs=[
            pl.BlockSpec((tm, d), lambda i: (i, 0)),
            pl.BlockSpec((ROW_BLOCK, d), lambda i: (jnp.maximum(i * nblk - 1, 0), 0)),
            pl.BlockSpec((ROW_BLOCK, d), lambda i: (jnp.minimum((i + 1) * nblk, last_blk), 0)),
            pl.BlockSpec((1, d), lambda i: (0, 0)),
            mod_spec, mod_spec,
            pl.BlockSpec((d, pw), lambda i: (0, 0), pipeline_mode=pl.Buffered(1)),
            full(mu), full(w0), full(w2cat), full(a0), full(a2cat), full(g2p), full(k_k), full(k_a),
            full(r_k), full(e),
        ],
        out_specs=[o1, o1, o1, o1, o1, o2, o2, o2],
        out_shape=[one, one, one, one, one, two, two, two],
        scratch_shapes=[pltpu.VMEM((tm + 2 * ROW_BLOCK, d), BF16), pltpu.VMEM((tm + 2 * ROW_BLOCK, pw), F32)],
        compiler_params=_cparams(("arbitrary",)),
    )(x, x, x, g.reshape(1, d), shift, scale, w_b, mu, w0, w2cat, a0, a2cat, g2p, k_k, k_a, r_k, e)


SCAN_CHUNK = 64
INV_BLOCK = 16


def _dot_nt(a, b):
    return lax.dot_general(a, b, (((1,), (1,)), ((), ())), preferred_element_type=F32)


def _dot_tn(a, b):
    return lax.dot_general(a, b, (((0,), (0,)), ((), ())), preferred_element_type=F32)


def _pair_blockdiag(x, lane_lo):
    return jnp.concatenate([jnp.where(lane_lo, x, 0.0), jnp.where(lane_lo, 0.0, x)], axis=0).astype(BF16)


def _chunk_scan_body(r_ref, v_ref, kk_ref, lw_ref, kka_ref, kd_ref, s0_ref, y_ref, sout_ref,
                     s_ref, qw_ref, u0_ref, arb_ref, pe_ref, vb_ref, elt_ref, *, reverse, nsub, pairs, group):
    c = SCAN_CHUNK
    step = pl.program_id(1)
    bf = lambda t: t.astype(BF16)

    @pl.when(step == 0)
    def _():
        s_ref[...] = s0_ref[...]

    ri = lax.broadcasted_iota(jnp.int32, (c, c), 0)
    ci = lax.broadcasted_iota(jnp.int32, (c, c), 1)
    tri = ((ri <= ci) if reverse else (ri >= ci)).astype(BF16)
    rp = lax.broadcasted_iota(jnp.int32, (c, LANE), 0)
    cp = lax.broadcasted_iota(jnp.int32, (c, LANE), 1) % c
    incl_p = (rp <= cp) if reverse else (rp >= cp)
    strict_p = (rp < cp) if reverse else (rp > cp)
    r2 = lax.broadcasted_iota(jnp.int32, (2 * c, LANE), 0)
    c2 = lax.broadcasted_iota(jnp.int32, (2 * c, LANE), 1) % c
    t2 = r2 % c
    mask_kr = ((t2 < c2) if reverse else (t2 > c2)) | ((r2 >= c) & (t2 == c2))
    blk_p = (rp // INV_BLOCK) == (cp // INV_BLOCK)
    eye_p = (rp == cp).astype(F32)
    lane_lo = lax.broadcasted_iota(jnp.int32, (c, LANE), 1) < HEAD_DIM
    bi = lax.broadcasted_iota(jnp.int32, (LANE, LANE), 0) // HEAD_DIM
    bj = lax.broadcasted_iota(jnp.int32, (LANE, LANE), 1) // HEAD_DIM
    same_head = bi == bj
    bd = lambda t: _pair_blockdiag(t, lane_lo)
    prange = range(pairs)
    lanes = [pl.ds(p * LANE, LANE) for p in prange]

    def phase_a(chunks):
        streams = [(j, p) for j in chunks for p in prange]
        sr = range(len(streams))
        rows = [pl.ds(j * c, c) for j, _ in streams]
        cols = [lanes[p] for _, p in streams]
        lw = [lw_ref[rows[s], cols[s]] for s in sr]
        lc = []
        for s in sr:
            hi = bf(lw[s])
            lc.append(_dot(tri, hi) + _dot(tri, bf(lw[s] - hi.astype(F32))))
        yield
        qq, qk_bd, qr, pk_bd, pb_bd, v_bd = [], [], [], [], [], []
        for s, (j, p) in enumerate(streams):
            ltot = lc[s][0:1] if reverse else lc[s][c - 1:c]
            e_neg = jnp.exp(-lc[s])
            e_end = jnp.exp(ltot - lc[s])
            kd = kd_ref[rows[s], cols[s]]
            kka = kka_ref[rows[s], cols[s]]
            v = v_ref[rows[s], cols[s]]
            qk = kk_ref[rows[s], cols[s]] * jnp.exp(lc[s] - lw[s])
            qr.append(r_ref[rows[s], cols[s]] * jnp.exp(lc[s]))
            qq.append(bf(jnp.concatenate([qk, qr[s]], axis=0)))
            qk_bd.append(bd(qk))
            pk_bd.append(bd(kd * e_neg))
            pb_bd.append(bd(kka * e_neg))
            v_bd.append(bd(v))
            pe_ref[j, p] = bf(jnp.concatenate([kd * e_end, -(kka * e_end)], axis=0))
            vb_ref[j, p] = bf(v)
            elt_ref[j, p] = jnp.broadcast_to(jnp.exp(ltot), (SUBLANE, LANE))
        g = [_dot_nt(qq[s], jnp.concatenate([pk_bd[s], pb_bd[s]], axis=0)) for s in sr]
        yield
        a_k = [bf(jnp.where(mask_kr, g[s][:, :LANE], 0.0)) for s in sr]
        nmat = [jnp.where(strict_p, g[s][:c, LANE:], 0.0) for s in sr]
        for s, (j, p) in enumerate(streams):
            arb_ref[j, p] = bf(jnp.where(incl_p, g[s][c:, LANE:], 0.0))
        avy = [_dot(a_k[s], v_bd[s]) for s in sr]
        av = [avy[s][:c] for s in sr]
        for s in sr:
            y_ref[rows[s], cols[s]] = avy[s][c:]
        nd = [jnp.where(blk_p, nmat[s], 0.0) for s in sr]
        lo_bd = [bd(nmat[s] - nd[s]) for s in sr]
        dinv = [eye_p - nd[s] for s in sr]
        mpow = [_dot(bf(-nd[s]), bd(-nd[s])) for s in sr]
        yield
        for _ in range(int(math.log2(INV_BLOCK)) - 2):
            both = [_dot(bf(jnp.concatenate([mpow[s], dinv[s]], axis=0)), bd(mpow[s])) for s in sr]
            mpow = [both[s][:c] for s in sr]
            dinv = [dinv[s] + both[s][c:] for s in sr]
            yield
        dinv = [dinv[s] + _dot(bf(dinv[s]), bd(mpow[s])) for s in sr]
        yield
        dinv_b = [bf(dinv[s]) for s in sr]
        x1 = [_dot(dinv_b[s], lo_bd[s]) for s in sr]
        yield
        x1b = [bf(x1[s]) for s in sr]
        acc = [eye_p - x1[s] for s in sr]
        xp = x1
        for k in range(2, SCAN_CHUNK // INV_BLOCK):
            xp = [_dot(x1b[s], bd(xp[s])) for s in sr]
            yield
            acc = [acc[s] + xp[s] if k % 2 == 0 else acc[s] - xp[s] for s in sr]
        tinv = [bf(_dot(bf(acc[s]), bd(dinv[s]))) for s in sr]
        yield
        wt = [_dot(tinv[s], qk_bd[s]) for s in sr]
        for s, (j, p) in enumerate(streams):
            u0_ref[j, p] = _dot(tinv[s], bd(av[s]))
            qw_ref[j, p] = bf(jnp.concatenate([wt[s], qr[s]], axis=0))

    def phase_b(chunks):
        for jj in chunks:
            rows = pl.ds(jj * c, c)
            s = [s_ref[p] for p in prange]
            xs = [_dot_nt(qw_ref[jj, p], bf(s[p])) for p in prange]
            yield
            u = [u0_ref[jj, p] + xs[p][:c] for p in prange]
            au = [_dot(arb_ref[jj, p], bd(u[p])) for p in prange]
            upd = [_dot_tn(jnp.concatenate([vb_ref[jj, p], bf(u[p])], axis=0), pe_ref[jj, p]) for p in prange]
            yield
            for p in prange:
                y_ref[rows, lanes[p]] = y_ref[rows, lanes[p]] + xs[p][c:] - au[p]
                s_ref[p] = s[p] * elt_ref[jj, p][0:1] + jnp.where(same_head, upd[p], 0.0)
            yield

    order = list(range(nsub))[::-1] if reverse else list(range(nsub))
    groups = [order[i:i + group] for i in range(0, nsub, group)]
    pending = None
    for chunks in groups + [None]:
        live = [gen for gen in (phase_a(chunks) if chunks is not None else None, pending) if gen is not None]
        while live:
            for gen in list(live):
                try:
                    next(gen)
                except StopIteration:
                    live.remove(gen)
        pending = phase_b(chunks) if chunks is not None else None

    @pl.when(step == pl.num_programs(1) - 1)
    def _():
        sout_ref[...] = s_ref[...]


def _chunk_scan(r, v, kk, lw, kka, kd, s0, *, bn, reverse, tt=512):
    m, bw = r.shape
    seq = m // bn
    tt = min(tt, seq)
    steps = seq // tt
    pairs = bw // LANE
    d = 1 if reverse else 0

    def row_blk(b, s):
        return b * steps + (steps - 1 - s if reverse else s)

    tok = pl.BlockSpec((tt, bw), lambda b, s: (row_blk(b, s), 0))
    tok_dir = pl.BlockSpec((tt, bw), lambda b, s: (row_blk(b, s), d))
    st = pl.BlockSpec((None, pairs, LANE, LANE), lambda b, s: (b, 0, 0, 0))
    c = SCAN_CHUNK
    nsub = tt // c
    group = 2 if nsub % 2 == 0 else 1
    body = functools.partial(_chunk_scan_body, reverse=reverse, nsub=nsub, pairs=pairs, group=group)
    return pl.pallas_call(
        body,
        grid=(bn, steps),
        in_specs=[tok, tok, tok, tok_dir, tok_dir, tok_dir, st],
        out_specs=[tok, st],
        out_shape=[jax.ShapeDtypeStruct((m, bw), F32), jax.ShapeDtypeStruct(s0.shape, F32)],
        scratch_shapes=[
            pltpu.VMEM((pairs, LANE, LANE), F32),
            pltpu.VMEM((nsub, pairs, 2 * c, LANE), BF16),
            pltpu.VMEM((nsub, pairs, c, LANE), F32),
            pltpu.VMEM((nsub, pairs, c, LANE), BF16),
            pltpu.VMEM((nsub, pairs, 2 * c, LANE), BF16),
            pltpu.VMEM((nsub, pairs, c, LANE), BF16),
            pltpu.VMEM((nsub, pairs, SUBLANE, LANE), F32),
        ],
        compiler_params=_cparams(("arbitrary", "arbitrary")),
    )(r, v, kk, lw, kka, kd, s0)


def _mixout_body(x_ref, gt_ref, ya_ref, yf_ref, yb_ref, bonus_ref, g_ref, lng_ref, lnb_ref, e_ref, wa_ref,
                 wb_ref, o_ref):
    e = e_ref[...]
    inv_n = 1.0 / HEAD_DIM
    y = yf_ref[...] + yb_ref[...]
    yc = y - _head_sums(y, e) * inv_n
    var = _head_sums(yc * yc, e) * inv_n
    yn = yc * lax.rsqrt(var + GN_EPS) * lng_ref[...] + lnb_ref[...]
    yb = ((yn + bonus_ref[...]) * g_ref[...]).astype(BF16)
    ox = _dot(ya_ref[...], wa_ref[...]) + _dot(yb, wb_ref[...])
    o_ref[...] = x_ref[...] + gt_ref[...] * ox


def _mix_out(x, gate, ya, y_fwd, y_bwd, bonus, g, ln_g, ln_b, e, w_out_a, w_out_b, *, rows_per_mod, tm=256):
    m, d = x.shape
    aw, bw = ya.shape[1], y_fwd.shape[1]
    tpb = rows_per_mod // tm
    full = lambda arr: pl.BlockSpec(arr.shape, lambda i: (0,) * arr.ndim)
    tok = lambda width: pl.BlockSpec((tm, width), lambda i: (i, 0))
    return pl.pallas_call(
        _mixout_body,
        grid=(m // tm,),
        in_specs=[
            tok(d), pl.BlockSpec((None, 1, d), lambda i: (i // tpb, 0, 0)),
            tok(aw), tok(bw), tok(bw), tok(bw), tok(bw),
            full(ln_g), full(ln_b), full(e), full(w_out_a), full(w_out_b),
        ],
        out_specs=tok(d),
        out_shape=jax.ShapeDtypeStruct((m, d), F32),
        compiler_params=_cparams(("arbitrary",)),
    )(x, gate, ya, y_fwd, y_bwd, bonus, g, ln_g, ln_b, e, w_out_a, w_out_b)


def kernel(x, c, ctx, c_ctx, ada_w, ada_b, norm_g, ffn_w1, ffn_w3, ffn_w2, ab_w_in, ab_w_out, gm_v_gain,
           gm_ws, gm_b, rw_mu, rw_w0, rw_w2, rw_a0, rw_a2, rw_g2, rw_k_k, rw_k_a, rw_r_k, rw_ln_g, rw_ln_b,
           sc_w_in, sc_conv, sc_w_out, final_g):
    bn, seq, d = x.shape
    ctx_len = ctx.shape[1]
    depth = ada_w.shape[0]
    n_mod = ada_w.shape[2] // d
    m_x, m_c = bn * seq, bn * ctx_len
    xs = x.reshape(m_x, d)
    cs = ctx.reshape(m_c, d)

    rows = _round_up(bn + 1, SUBLANE)
    c_rows = jnp.zeros((rows, d), F32).at[:bn].set(c).at[bn].set(c_ctx)
    mods = _modulation(c_rows, ada_w, ada_b).reshape(depth, rows, n_mod, d)

    for i in range(depth):
        even = i % 2 == 0
        ctx_after = any(j % 2 == 0 for j in range(i + 1, depth))
        use_ctx = cs is not None and (even or ctx_after)
        if not use_ctx:
            cs = None
        mx = [mods[i, :bn, j].reshape(bn, 1, d) for j in range(n_mod)]
        mc = [mods[i, bn:bn + 1, j].reshape(1, 1, d) for j in range(n_mod)]
        w1 = ffn_w1[i].astype(BF16)
        w3 = ffn_w3[i].astype(BF16)
        w2 = ffn_w2[i].astype(BF16)
        last_layer = i == depth - 1

        xs = _glu_block(xs, norm_g[i, 0], mx[0], mx[1], mx[2], [w1[0], w3[0]], w2[0], kind="swiglu",
                        coef=0.5, rows_per_mod=seq)
        if cs is not None:
            cs = _glu_block(cs, norm_g[i, 0], mc[0], mc[1], mc[2], [w1[0], w3[0]], w2[0], kind="swiglu",
                            coef=0.5, rows_per_mod=m_c)

        if even:
            e_idx = i // 2
            if ctx_after:
                raise NotImplementedError("context output of an even layer is only needed for depth > 2")
            xs = _ab_mixer(xs, cs, mx, mc, norm_g[i, 1], bn, seq, ctx_len, ab_w_in[e_idx], ab_w_out[e_idx],
                           gm_v_gain[e_idx], gm_ws[e_idx], gm_b[e_idx], rw_mu[e_idx], rw_w0[e_idx],
                           rw_w2[e_idx], rw_a0[e_idx], rw_a2[e_idx], rw_g2[e_idx], rw_k_k[e_idx],
                           rw_k_a[e_idx], rw_r_k[e_idx], rw_ln_g[e_idx], rw_ln_b[e_idx])
        else:
            o_idx = i // 2
            if ctx_after:
                raise NotImplementedError("context output of an odd layer is only needed for depth > 2")
            w_in = sc_w_in[o_idx].astype(BF16)
            ups = [w_in[:, 0:d], w_in[:, d:2 * d], w_in[:, 2 * d:3 * d]]
            xs = _glu_block(xs, norm_g[i, 1], mx[3], mx[4], mx[5], ups, sc_w_out[o_idx].astype(BF16),
                            kind="conv", coef=1.0, conv_w=sc_conv[o_idx], rows_per_mod=seq, tm=512)
        cs = None

        xs = _glu_block(xs, norm_g[i, 2], mx[6], mx[7], mx[8], [w1[1], w3[1]], w2[1], kind="swiglu",
                        coef=0.5, final_g=final_g if last_layer else None, rows_per_mod=seq)
    return xs.reshape(bn, seq, d)


def _ab_mixer(xs, cs, mx, mc, g, bn, seq, ctx_len, w_in, w_out, v_gain, ws, b_s, mu, w0, w2, a0, a2, g2,
              k_k, k_a, r_k, ln_g, ln_b):
    m_x, d = xs.shape
    m_c = cs.shape[0]
    bw = k_k.shape[0]
    aw = v_gain.shape[0]
    a_cols = 2 * aw
    heads = bw // HEAD_DIM
    wl, al, gl = w2.shape[1], a2.shape[1], g2.shape[0]
    b_cols = w_in.shape[1] - a_cols
    pw = _round_up(b_cols, 512)
    gp = pw - (3 * bw + 2 * wl + 2 * al)

    w_in_bf = w_in.astype(BF16)
    w_in_a = w_in_bf[:, :a_cols]
    w_in_b = jnp.pad(w_in_bf[:, a_cols:], ((0, 0), (0, pw - b_cols)))

    mu_p = jnp.pad(mu, ((0, 0), (0, pw - b_cols)))
    zeros_w = jnp.zeros((wl, bw), F32)
    zeros_a = jnp.zeros((al, bw), F32)
    w2cat = jnp.concatenate([jnp.concatenate([w2[0], zeros_w], 1), jnp.concatenate([zeros_w, w2[1]], 1)], 0)
    a2cat = jnp.concatenate([jnp.concatenate([a2[0], zeros_a], 1), jnp.concatenate([zeros_a, a2[1]], 1)], 0)
    g2p = jnp.pad(g2, ((0, gp - gl), (0, 0)))
    head_id = jnp.arange(min(bw, MXU_WIDTH)) // HEAD_DIM
    e = (head_id[:, None] == head_id[None, :]).astype(BF16)
    consts = (mu_p, w0.reshape(1, 2 * bw), w2cat.astype(BF16), a0.reshape(1, 2 * bw), a2cat.astype(BF16),
              g2p.astype(BF16), k_k.reshape(1, bw), k_a.reshape(1, bw), r_k.reshape(1, bw), e)

    b_full = jnp.repeat(b_s.T, aw // ws.shape[0], axis=1)
    ya = _gmlp_branch(xs, g, mx[3], mx[4], w_in_a, v_gain, ws.astype(BF16), b_full, rows_per_mod=seq)

    fx = _rwkv_front(xs, g, mx[3], mx[4], w_in_b, seq, consts, rows_per_mod=seq)
    fc = _rwkv_front(cs, g, mc[3], mc[4], w_in_b, ctx_len, consts, rows_per_mod=m_c)
    r_x, v_x, kk_x, g_x, bonus_x, lw_x, kka_x, kd_x = fx
    r_c, v_c, kk_c, _, _, lw_c, kka_c, kd_c = fc
    s_zero = jnp.zeros((bn, bw // LANE, LANE, LANE), F32)
    ys = []
    for reverse in (False, True):
        _, s_ctx = _chunk_scan(r_c, v_c, kk_c, lw_c, kka_c, kd_c, s_zero, bn=bn, reverse=reverse)
        y_dir, _ = _chunk_scan(r_x, v_x, kk_x, lw_x, kka_x, kd_x, s_ctx, bn=bn, reverse=reverse)
        ys.append(y_dir)

    w_out_bf = w_out.astype(BF16)
    return _mix_out(xs, mx[5], ya, ys[0], ys[1], bonus_x, g_x, ln_g.reshape(1, bw), ln_b.reshape(1, bw), e,
                    w_out_bf[:aw], w_out_bf[aw:], rows_per_mod=seq)
```

```python
import functools
import math

import jax
import jax.numpy as jnp
from jax import lax
from jax.experimental import pallas as pl
from jax.experimental.pallas import tpu as pltpu

F32 = jnp.float32
BF16 = jnp.bfloat16

GRID_W = 64
CHUNK = 128
HEAD_DIM = 64
RMS_EPS = 1e-6
LN_EPS = 1e-5
GN_EPS = 64e-5
KK_EPS = 1e-12
W_DECAY_SCALE = math.exp(-0.5)

LANE = 128
SUBLANE = 8
MXU_WIDTH = 256
VMEM_LIMIT = 56 * 1024 * 1024


def _cparams(sem):
    return pltpu.CompilerParams(dimension_semantics=sem, vmem_limit_bytes=VMEM_LIMIT)


def _round_up(n, m):
    return (n + m - 1) // m * m


def _dot(a, b):
    return jnp.dot(a, b, preferred_element_type=F32)


def _dot2(a, b):
    hi = a.astype(BF16)
    lo = (a - hi.astype(F32)).astype(BF16)
    return _dot(hi, b) + _dot(lo, b)


def _head_sums(x, e):
    blk = e.shape[0]
    parts = [_dot2(x[:, j:j + blk], e) for j in range(0, x.shape[1], blk)]
    return parts[0] if len(parts) == 1 else jnp.concatenate(parts, axis=1)


ROW_BLOCK = 16
ROW_UNROLL = 8
EPILOGUE_ROWS = 128


def _prenorm_rows(x_ref, g, shift, scale, h_ref, *, src_row=0, dst_row=0, n_rows=None):
    n_rows = x_ref.shape[0] if n_rows is None else n_rows
    gs = g * (1.0 + scale)

    def block(i, carry):
        off = pl.multiple_of(i * ROW_BLOCK, ROW_BLOCK)
        x = x_ref[pl.ds(src_row + off, ROW_BLOCK), :]
        ms = jnp.mean(x * x, axis=-1, keepdims=True)
        h_ref[pl.ds(dst_row + off, ROW_BLOCK), :] = (x * lax.rsqrt(ms + RMS_EPS) * gs + shift).astype(h_ref.dtype)
        return carry

    n_blocks = n_rows // ROW_BLOCK
    lax.fori_loop(0, n_blocks, block, 0, unroll=math.gcd(n_blocks, ROW_UNROLL))


def _mod_body(c_ref, w_ref, b_ref, o_ref):
    s = jax.nn.silu(c_ref[...]).astype(BF16)
    o_ref[...] = _dot(s, w_ref[...].astype(BF16)) + b_ref[...]


def _modulation(c_rows, ada_w, ada_b):
    depth, d, n = ada_w.shape
    tn = 1024 if n % 1024 == 0 else 512
    rows = c_rows.shape[0]
    return pl.pallas_call(
        _mod_body,
        grid=(depth, n // tn),
        in_specs=[
            pl.BlockSpec((rows, d), lambda l, j: (0, 0)),
            pl.BlockSpec((None, d, tn), lambda l, j: (l, 0, j)),
            pl.BlockSpec((None, 1, tn), lambda l, j: (l, 0, j)),
        ],
        out_specs=pl.BlockSpec((None, rows, tn), lambda l, j: (l, 0, j)),
        out_shape=jax.ShapeDtypeStruct((depth, rows, n), F32),
        compiler_params=_cparams(("arbitrary", "arbitrary")),
    )(c_rows, ada_w, ada_b.reshape(depth, 1, n))


def _glu_body(*refs, n_up, kind, coef, final, nf):
    x_ref, g_ref, sh_ref, sc_ref, gt_ref = refs[:5]
    up_refs = refs[5:5 + n_up]
    down_ref = refs[5 + n_up]
    idx = 6 + n_up
    cw_ref = fg_ref = None
    if kind == "conv":
        cw_ref = refs[idx]
        idx += 1
    if final:
        fg_ref = refs[idx]
        idx += 1
    o_ref, h_ref = refs[idx], refs[idx + 1]
    f = pl.program_id(1)

    @pl.when(f == 0)
    def _():
        _prenorm_rows(x_ref, g_ref[...], sh_ref[...], sc_ref[...], h_ref)
        o_ref[...] = jnp.zeros_like(o_ref)

    h = h_ref[...]
    ups = [_dot(h, u[...]) for u in up_refs]
    if kind == "swiglu":
        mid = jax.nn.silu(ups[0]) * ups[1]
    else:
        z = ups[1] * ups[2]
        rows = z.shape[0]
        col = lax.broadcasted_iota(jnp.int32, z.shape, 0) % GRID_W
        zp = jnp.where(col == 0, 0.0, pltpu.roll(z, 1, axis=0))
        zn = jnp.where(col == GRID_W - 1, 0.0, pltpu.roll(z, rows - 1, axis=0))
        cw = cw_ref[...]
        mid = ups[0] * (cw[0:1] * zp + cw[1:2] * z + cw[2:3] * zn)
    o_ref[...] += _dot(mid.astype(BF16), down_ref[...])

    @pl.when(f == nf - 1)
    def _():
        cg = coef * gt_ref[...]
        tm = o_ref.shape[0]
        step = min(tm, EPILOGUE_ROWS)
        for r0 in range(0, tm, step):
            rows = slice(r0, r0 + step)
            res = x_ref[rows, :] + cg * o_ref[rows, :]
            if final:
                ms = jnp.mean(res * res, axis=-1, keepdims=True)
                res = res * lax.rsqrt(ms + RMS_EPS) * fg_ref[...]
            o_ref[rows, :] = res


def _glu_block(x, g, shift, scale, gate, ups, down, *, kind, coef, conv_w=None, final_g=None,
               rows_per_mod, tm=1024, tf=512):
    m, d = x.shape
    down_w, down_lead = down
    fdim = down_w.shape[-2]
    tm = min(tm, m, rows_per_mod)
    nf = fdim // tf
    tpb = rows_per_mod // tm
    mod_spec = pl.BlockSpec((None, 1, d), lambda i, f: (i // tpb, 0, 0))
    in_specs = [
        pl.BlockSpec((tm, d), lambda i, f: (i, 0)),
        pl.BlockSpec((1, d), lambda i, f: (0, 0)),
        mod_spec, mod_spec, mod_spec,
    ]
    args = [x, g.reshape(1, d), shift, scale, gate]
    for w, lead, col0 in ups:
        blk0 = col0 // tf
        in_specs.append(pl.BlockSpec((None,) * len(lead) + (d, tf),
                                     lambda i, f, lead=lead, blk0=blk0: lead + (0, blk0 + f)))
        args.append(w)
    in_specs.append(pl.BlockSpec((None,) * len(down_lead) + (tf, d), lambda i, f: down_lead + (f, 0)))
    args.append(down_w)
    if kind == "conv":
        cw, cw_lead = conv_w
        in_specs.append(pl.BlockSpec((None,) * len(cw_lead) + (3, tf), lambda i, f: cw_lead + (0, f)))
        args.append(cw)
    if final_g is not None:
        in_specs.append(pl.BlockSpec((1, d), lambda i, f: (0, 0)))
        args.append(final_g.reshape(1, d))
    body = functools.partial(_glu_body, n_up=len(ups), kind=kind, coef=coef,
                             final=final_g is not None, nf=nf)
    return pl.pallas_call(
        body,
        grid=(m // tm, nf),
        in_specs=in_specs,
        out_specs=pl.BlockSpec((tm, d), lambda i, f: (i, 0)),
        out_shape=jax.ShapeDtypeStruct((m, d), F32),
        scratch_shapes=[pltpu.VMEM((tm, d), BF16)],
        compiler_params=_cparams(("arbitrary", "arbitrary")),
    )(*args)


def _gmlp_body(x_ref, g_ref, sh_ref, sc_ref, w_ref, gain_ref, ws_ref, b_ref, o_ref, h_ref, p_ref, *,
               aw, groups, n_chunks):
    _prenorm_rows(x_ref, g_ref[...], sh_ref[...], sc_ref[...], h_ref)
    p_ref[...] = _dot(h_ref[...], w_ref[...])
    gd = aw // groups
    for c in range(n_chunks):
        rows = pl.ds(c * CHUNK, CHUNK)
        u = jax.nn.gelu(p_ref[rows, 0:aw])
        v = jax.nn.gelu(p_ref[rows, aw:2 * aw])
        vm = jnp.mean(v, axis=-1, keepdims=True)
        vc = v - vm
        vv = jnp.mean(vc * vc, axis=-1, keepdims=True)
        vn = (vc * lax.rsqrt(vv + LN_EPS) * gain_ref[...]).astype(BF16)
        for gi in range(groups):
            cols = slice(gi * gd, (gi + 1) * gd)
            vs = _dot(ws_ref[gi], vn[:, cols]) + b_ref[:, cols]
            o_ref[rows, cols] = (u[:, cols] * vs).astype(o_ref.dtype)


def _gmlp_branch(x, g, shift, scale, w_a, gain, ws, b_full, *, rows_per_mod, tm=512):
    m, d = x.shape
    two_aw = w_a.shape[1]
    aw = two_aw // 2
    groups = ws.shape[0]
    tm = min(tm, m, rows_per_mod)
    tpb = rows_per_mod // tm
    mod_spec = pl.BlockSpec((None, 1, d), lambda i: (i // tpb, 0, 0))
    body = functools.partial(_gmlp_body, aw=aw, groups=groups, n_chunks=tm // CHUNK)
    return pl.pallas_call(
        body,
        grid=(m // tm,),
        in_specs=[
            pl.BlockSpec((tm, d), lambda i: (i, 0)),
            pl.BlockSpec((1, d), lambda i: (0, 0)),
            mod_spec, mod_spec,
            pl.BlockSpec((d, two_aw), lambda i: (0, 0), pipeline_mode=pl.Buffered(1)),
            pl.BlockSpec((1, aw), lambda i: (0, 0)),
            pl.BlockSpec((groups, CHUNK, CHUNK), lambda i: (0, 0, 0)),
            pl.BlockSpec((CHUNK, aw), lambda i: (0, 0)),
        ],
        out_specs=pl.BlockSpec((tm, aw), lambda i: (i, 0)),
        out_shape=jax.ShapeDtypeStruct((m, aw), BF16),
        scratch_shapes=[pltpu.VMEM((tm, d), BF16), pltpu.VMEM((tm, two_aw), F32)],
        compiler_params=_cparams(("arbitrary",)),
    )(x, g.reshape(1, d), shift, scale, w_a, gain.reshape(1, aw), ws, b_full)


def _prep_body(x_ref, xp_ref, xn_ref, g_ref, sh_ref, sc_ref, w_ref, mu_ref, w0_ref, w2_ref, a0_ref, a2_ref,
               g2_ref, kk_ref, ka_ref, rk_ref, e_ref, r_o, v_o, kk_o, g_o, bonus_o, lw_o, kka_o, kd_o,
               h_ref, p_ref, *, bw, wl2, al2, seq, tm):
    i = pl.program_id(0)
    g, sh, sc = g_ref[...], sh_ref[...], sc_ref[...]
    _prenorm_rows(x_ref, g, sh, sc, h_ref)
    _prenorm_rows(xp_ref, g, sh, sc, h_ref, dst_row=tm)
    _prenorm_rows(xn_ref, g, sh, sc, h_ref, dst_row=tm + ROW_BLOCK)
    p_ref[...] = _dot(h_ref[...], w_ref[...])

    p = p_ref[0:tm, :]
    row = lax.broadcasted_iota(jnp.int32, p.shape, 0)
    first = (i * tm) % seq == 0
    last = ((i + 1) * tm) % seq == 0
    prev_row = jnp.where(first, 0.0, p_ref[tm + ROW_BLOCK - 1:tm + ROW_BLOCK, :])
    next_row = jnp.where(last, 0.0, p_ref[tm + ROW_BLOCK:tm + ROW_BLOCK + 1, :])
    prev = jnp.where(row == 0, prev_row, pltpu.roll(p, 1, axis=0))
    nxt = jnp.where(row == tm - 1, next_row, pltpu.roll(p, tm - 1, axis=0))
    mu = mu_ref[...]
    p = p + mu[0:1] * (prev - p) + mu[1:2] * (nxt - p)

    r = p[:, 0:bw]
    k = p[:, bw:2 * bw]
    v = p[:, 2 * bw:3 * bw]
    o = 3 * bw
    wd = jnp.tanh(p[:, o:o + wl2]).astype(BF16)
    ad = p[:, o + wl2:o + wl2 + al2].astype(BF16)
    gdn = jax.nn.sigmoid(p[:, o + wl2 + al2:]).astype(BF16)

    lw = -W_DECAY_SCALE * jax.nn.sigmoid(w0_ref[...] + _dot(wd, w2_ref[...]))
    a = jax.nn.sigmoid(a0_ref[...] + _dot(ad, a2_ref[...]))
    gate = _dot(gdn, g2_ref[...])

    e = e_ref[...]
    kq = k * kk_ref[...]
    kk = kq * lax.rsqrt(_head_sums(kq * kq, e) + KK_EPS)
    ka = ka_ref[...]
    kd0 = k * (1.0 + (a[:, 0:bw] - 1.0) * ka)
    kd1 = k * (1.0 + (a[:, bw:] - 1.0) * ka)
    kb = 0.5 * (kd0 + kd1)
    bonus = _head_sums(r * kb * rk_ref[...], e) * v

    r_o[...] = r
    v_o[...] = v
    kk_o[...] = kk
    g_o[...] = gate
    bonus_o[...] = bonus
    lw_o[...] = lw
    kka_o[:, 0:bw] = kk * a[:, 0:bw]
    kka_o[:, bw:] = kk * a[:, bw:]
    kd_o[:, 0:bw] = kd0
    kd_o[:, bw:] = kd1


def _rwkv_front(x, g, shift, scale, w_b, seq, consts, *, rows_per_mod, tm=256):
    m, d = x.shape
    pw = w_b.shape[1]
    (mu, w0, w2cat, a0, a2cat, g2p, k_k, k_a, r_k, e) = consts
    bw = k_k.shape[1]
    wl2, al2 = w2cat.shape[0], a2cat.shape[0]
    tm = min(tm, seq)
    tpb = rows_per_mod // tm
    nblk = tm // ROW_BLOCK
    last_blk = m // ROW_BLOCK - 1
    full = lambda arr: pl.BlockSpec(arr.shape, lambda i: (0,) * arr.ndim)
    mod_spec = pl.BlockSpec((None, 1, d), lambda i: (i // tpb, 0, 0))
    body = functools.partial(_prep_body, bw=bw, wl2=wl2, al2=al2, seq=seq, tm=tm)
    one = jax.ShapeDtypeStruct((m, bw), F32)
    two = jax.ShapeDtypeStruct((m, 2 * bw), F32)
    o1 = pl.BlockSpec((tm, bw), lambda i: (i, 0))
    o2 = pl.BlockSpec((tm, 2 * bw), lambda i: (i, 0))
    return pl.pallas_call(
        body,
        grid=(m // tm,),
        in_specs=[
            pl.BlockSpec((tm, d), lambda i: (i, 0)),
            pl.BlockSpec((ROW_BLOCK, d), lambda i: (jnp.maximum(i * nblk - 1, 0), 0)),
            pl.BlockSpec((ROW_BLOCK, d), lambda i: (jnp.minimum((i + 1) * nblk, last_blk), 0)),
            pl.BlockSpec((1, d), lambda i: (0, 0)),
            mod_spec, mod_spec,
            pl.BlockSpec((d, pw), lambda i: (0, 0), pipeline_mode=pl.Buffered(1)),
            full(mu), full(w0), full(w2cat), full(a0), full(a2cat), full(g2p), full(k_k), full(k_a),
            full(r_k), full(e),
        ],
        out_specs=[o1, o1, o1, o1, o1, o2, o2, o2],
        out_shape=[one, one, one, one, one, two, two, two],
        scratch_shapes=[pltpu.VMEM((tm + 2 * ROW_BLOCK, d), BF16), pltpu.VMEM((tm + 2 * ROW_BLOCK, pw), F32)],
        compiler_params=_cparams(("arbitrary",)),
    )(x, x, x, g.reshape(1, d), shift, scale, w_b, mu, w0, w2cat, a0, a2cat, g2p, k_k, k_a, r_k, e)


SCAN_CHUNK = 64
INV_BLOCK = 16


def _dot_nt(a, b):
    return lax.dot_general(a, b, (((1,), (1,)), ((), ())), preferred_element_type=F32)


def _dot_tn(a, b):
    return lax.dot_general(a, b, (((0,), (0,)), ((), ())), preferred_element_type=F32)


def _pair_blockdiag(x, lane_lo):
    return jnp.concatenate([jnp.where(lane_lo, x, 0.0), jnp.where(lane_lo, 0.0, x)], axis=0).astype(BF16)


def _chunk_scan_body(r_ref, v_ref, kk_ref, lw_ref, kka_ref, kd_ref, s0_ref, y_ref, sout_ref,
                     s_ref, qw_ref, u0_ref, arb_ref, pe_ref, vb_ref, elt_ref, *, reverse, nsub, pairs, group):
    c = SCAN_CHUNK
    step = pl.program_id(1)
    bf = lambda t: t.astype(BF16)

    @pl.when(step == 0)
    def _():
        s_ref[...] = s0_ref[...]

    ri = lax.broadcasted_iota(jnp.int32, (c, c), 0)
    ci = lax.broadcasted_iota(jnp.int32, (c, c), 1)
    tri = ((ri <= ci) if reverse else (ri >= ci)).astype(BF16)
    rp = lax.broadcasted_iota(jnp.int32, (c, LANE), 0)
    cp = lax.broadcasted_iota(jnp.int32, (c, LANE), 1) % c
    incl_p = (rp <= cp) if reverse else (rp >= cp)
    strict_p = (rp < cp) if reverse else (rp > cp)
    r2 = lax.broadcasted_iota(jnp.int32, (2 * c, LANE), 0)
    c2 = lax.broadcasted_iota(jnp.int32, (2 * c, LANE), 1) % c
    t2 = r2 % c
    mask_kr = ((t2 < c2) if reverse else (t2 > c2)) | ((r2 >= c) & (t2 == c2))
    blk_p = (rp // INV_BLOCK) == (cp // INV_BLOCK)
    eye_p = (rp == cp).astype(F32)
    lane_lo = lax.broadcasted_iota(jnp.int32, (c, LANE), 1) < HEAD_DIM
    bi = lax.broadcasted_iota(jnp.int32, (LANE, LANE), 0) // HEAD_DIM
    bj = lax.broadcasted_iota(jnp.int32, (LANE, LANE), 1) // HEAD_DIM
    same_head = bi == bj
    bd = lambda t: _pair_blockdiag(t, lane_lo)
    prange = range(pairs)
    lanes = [pl.ds(p * LANE, LANE) for p in prange]

    def phase_a(chunks):
        streams = [(j, p) for j in chunks for p in prange]
        sr = range(len(streams))
        rows = [pl.ds(j * c, c) for j, _ in streams]
        cols = [lanes[p] for _, p in streams]
        lw = [lw_ref[rows[s], cols[s]] for s in sr]
        lc = []
        for s in sr:
            hi = bf(lw[s])
            lc.append(_dot(tri, hi) + _dot(tri, bf(lw[s] - hi.astype(F32))))
        yield
        qq, qk_bd, qr, pk_bd, pb_bd, v_bd = [], [], [], [], [], []
        for s, (j, p) in enumerate(streams):
            ltot = lc[s][0:1] if reverse else lc[s][c - 1:c]
            e_neg = jnp.exp(-lc[s])
            e_end = jnp.exp(ltot - lc[s])
            kd = kd_ref[rows[s], cols[s]]
            kka = kka_ref[rows[s], cols[s]]
            v = v_ref[rows[s], cols[s]]
            qk = kk_ref[rows[s], cols[s]] * jnp.exp(lc[s] - lw[s])
            qr.append(r_ref[rows[s], cols[s]] * jnp.exp(lc[s]))
            qq.append(bf(jnp.concatenate([qk, qr[s]], axis=0)))
            qk_bd.append(bd(qk))
            pk_bd.append(bd(kd * e_neg))
            pb_bd.append(bd(kka * e_neg))
            v_bd.append(bd(v))
            pe_ref[j, p] = bf(jnp.concatenate([kd * e_end, -(kka * e_end)], axis=0))
            vb_ref[j, p] = bf(v)
            elt_ref[j, p] = jnp.broadcast_to(jnp.exp(ltot), (SUBLANE, LANE))
        g = [_dot_nt(qq[s], jnp.concatenate([pk_bd[s], pb_bd[s]], axis=0)) for s in sr]
        yield
        a_k = [bf(jnp.where(mask_kr, g[s][:, :LANE], 0.0)) for s in sr]
        nmat = [jnp.where(strict_p, g[s][:c, LANE:], 0.0) for s in sr]
        for s, (j, p) in enumerate(streams):
            arb_ref[j, p] = bf(jnp.where(incl_p, g[s][c:, LANE:], 0.0))
        avy = [_dot(a_k[s], v_bd[s]) for s in sr]
        av = [avy[s][:c] for s in sr]
        for s in sr:
            y_ref[rows[s], cols[s]] = avy[s][c:]
        nd = [jnp.where(blk_p, nmat[s], 0.0) for s in sr]
        lo_bd = [bd(nmat[s] - nd[s]) for s in sr]
        dinv = [eye_p - nd[s] for s in sr]
        mpow = [_dot(bf(-nd[s]), bd(-nd[s])) for s in sr]
        yield
        for _ in range(int(math.log2(INV_BLOCK)) - 2):
            both = [_dot(bf(jnp.concatenate([mpow[s], dinv[s]], axis=0)), bd(mpow[s])) for s in sr]
            mpow = [both[s][:c] for s in sr]
            dinv = [dinv[s] + both[s][c:] for s in sr]
            yield
        dinv = [dinv[s] + _dot(bf(dinv[s]), bd(mpow[s])) for s in sr]
        yield
        dinv_b = [bf(dinv[s]) for s in sr]
        x1 = [_dot(dinv_b[s], lo_bd[s]) for s in sr]
        yield
        x1b = [bf(x1[s]) for s in sr]
        acc = [eye_p - x1[s] for s in sr]
        xp = x1
        for k in range(2, SCAN_CHUNK // INV_BLOCK):
            xp = [_dot(x1b[s], bd(xp[s])) for s in sr]
            yield
            acc = [acc[s] + xp[s] if k % 2 == 0 else acc[s] - xp[s] for s in sr]
        tinv = [bf(_dot(bf(acc[s]), bd(dinv[s]))) for s in sr]
        yield
        wt = [_dot(tinv[s], qk_bd[s]) for s in sr]
        for s, (j, p) in enumerate(streams):
            u0_ref[j, p] = _dot(tinv[s], bd(av[s]))
            qw_ref[j, p] = bf(jnp.concatenate([wt[s], qr[s]], axis=0))

    def phase_b(chunks):
        for jj in chunks:
            rows = pl.ds(jj * c, c)
            s = [s_ref[p] for p in prange]
            xs = [_dot_nt(qw_ref[jj, p], bf(s[p])) for p in prange]
            yield
            u = [u0_ref[jj, p] + xs[p][:c] for p in prange]
            au = [_dot(arb_ref[jj, p], bd(u[p])) for p in prange]
            upd = [_dot_tn(jnp.concatenate([vb_ref[jj, p], bf(u[p])], axis=0), pe_ref[jj, p]) for p in prange]
            yield
            for p in prange:
                y_ref[rows, lanes[p]] = y_ref[rows, lanes[p]] + xs[p][c:] - au[p]
                s_ref[p] = s[p] * elt_ref[jj, p][0:1] + jnp.where(same_head, upd[p], 0.0)
            yield

    order = list(range(nsub))[::-1] if reverse else list(range(nsub))
    groups = [order[i:i + group] for i in range(0, nsub, group)]
    pending = None
    for chunks in groups + [None]:
        live = [gen for gen in (phase_a(chunks) if chunks is not None else None, pending) if gen is not None]
        while live:
            for gen in list(live):
                try:
                    next(gen)
                except StopIteration:
                    live.remove(gen)
        pending = phase_b(chunks) if chunks is not None else None

    @pl.when(step == pl.num_programs(1) - 1)
    def _():
        sout_ref[...] = s_ref[...]


def _chunk_scan(r, v, kk, lw, kka, kd, s0, *, bn, reverse, tt=512):
    m, bw = r.shape
    seq = m // bn
    tt = min(tt, seq)
    steps = seq // tt
    pairs = bw // LANE
    d = 1 if reverse else 0

    def row_blk(b, s):
        return b * steps + (steps - 1 - s if reverse else s)

    tok = pl.BlockSpec((tt, bw), lambda b, s: (row_blk(b, s), 0))
    tok_dir = pl.BlockSpec((tt, bw), lambda b, s: (row_blk(b, s), d))
    st = pl.BlockSpec((None, pairs, LANE, LANE), lambda b, s: (b, 0, 0, 0))
    c = SCAN_CHUNK
    nsub = tt // c
    group = 2 if nsub % 2 == 0 else 1
    body = functools.partial(_chunk_scan_body, reverse=reverse, nsub=nsub, pairs=pairs, group=group)
    return pl.pallas_call(
        body,
        grid=(bn, steps),
        in_specs=[tok, tok, tok, tok_dir, tok_dir, tok_dir, st],
        out_specs=[tok, st],
        out_shape=[jax.ShapeDtypeStruct((m, bw), F32), jax.ShapeDtypeStruct(s0.shape, F32)],
        scratch_shapes=[
            pltpu.VMEM((pairs, LANE, LANE), F32),
            pltpu.VMEM((nsub, pairs, 2 * c, LANE), BF16),
            pltpu.VMEM((nsub, pairs, c, LANE), F32),
            pltpu.VMEM((nsub, pairs, c, LANE), BF16),
            pltpu.VMEM((nsub, pairs, 2 * c, LANE), BF16),
            pltpu.VMEM((nsub, pairs, c, LANE), BF16),
            pltpu.VMEM((nsub, pairs, SUBLANE, LANE), F32),
        ],
        compiler_params=_cparams(("arbitrary", "arbitrary")),
    )(r, v, kk, lw, kka, kd, s0)


def _mixout_body(x_ref, gt_ref, ya_ref, yf_ref, yb_ref, bonus_ref, g_ref, lng_ref, lnb_ref, e_ref, wa_ref,
                 wb_ref, o_ref):
    e = e_ref[...]
    inv_n = 1.0 / HEAD_DIM
    y = yf_ref[...] + yb_ref[...]
    yc = y - _head_sums(y, e) * inv_n
    var = _head_sums(yc * yc, e) * inv_n
    yn = yc * lax.rsqrt(var + GN_EPS) * lng_ref[...] + lnb_ref[...]
    yb = ((yn + bonus_ref[...]) * g_ref[...]).astype(BF16)
    ox = _dot(ya_ref[...], wa_ref[...]) + _dot(yb, wb_ref[...])
    o_ref[...] = x_ref[...] + gt_ref[...] * ox


def _mix_out(x, gate, ya, y_fwd, y_bwd, bonus, g, ln_g, ln_b, e, w_out_a, w_out_b, *, rows_per_mod, tm=256):
    m, d = x.shape
    aw, bw = ya.shape[1], y_fwd.shape[1]
    tpb = rows_per_mod // tm
    full = lambda arr: pl.BlockSpec(arr.shape, lambda i: (0,) * arr.ndim)
    tok = lambda width: pl.BlockSpec((tm, width), lambda i: (i, 0))
    return pl.pallas_call(
        _mixout_body,
        grid=(m // tm,),
        in_specs=[
            tok(d), pl.BlockSpec((None, 1, d), lambda i: (i // tpb, 0, 0)),
            tok(aw), tok(bw), tok(bw), tok(bw), tok(bw),
            full(ln_g), full(ln_b), full(e), full(w_out_a), full(w_out_b),
        ],
        out_specs=tok(d),
        out_shape=jax.ShapeDtypeStruct((m, d), F32),
        compiler_params=_cparams(("arbitrary",)),
    )(x, gate, ya, y_fwd, y_bwd, bonus, g, ln_g, ln_b, e, w_out_a, w_out_b)


def kernel(x, c, ctx, c_ctx, ada_w, ada_b, norm_g, ffn_w1, ffn_w3, ffn_w2, ab_w_in, ab_w_out, gm_v_gain,
           gm_ws, gm_b, rw_mu, rw_w0, rw_w2, rw_a0, rw_a2, rw_g2, rw_k_k, rw_k_a, rw_r_k, rw_ln_g, rw_ln_b,
           sc_w_in, sc_conv, sc_w_out, final_g):
    bn, seq, d = x.shape
    ctx_len = ctx.shape[1]
    depth = ada_w.shape[0]
    n_mod = ada_w.shape[2] // d
    m_x, m_c = bn * seq, bn * ctx_len
    xs = x.reshape(m_x, d)
    cs = ctx.reshape(m_c, d)

    rows = _round_up(bn + 1, SUBLANE)
    c_rows = jnp.zeros((rows, d), F32).at[:bn].set(c).at[bn].set(c_ctx)
    mods = _modulation(c_rows, ada_w, ada_b).reshape(depth, rows, n_mod, d)

    w1, w3, w2 = ffn_w1.astype(BF16), ffn_w3.astype(BF16), ffn_w2.astype(BF16)
    sc_in, sc_out = sc_w_in.astype(BF16), sc_w_out.astype(BF16)

    def ffn(t, i, half, sub, mod, rows_per_mod, final=None):
        return _glu_block(t, norm_g[i, sub], mod[3 * sub], mod[3 * sub + 1], mod[3 * sub + 2],
                          [(w1, (i, half), 0), (w3, (i, half), 0)], (w2, (i, half)), kind="swiglu", coef=0.5,
                          final_g=final, rows_per_mod=rows_per_mod, tf=512 if final is None else 256)

    for i in range(depth):
        even = i % 2 == 0
        ctx_after = any(j % 2 == 0 for j in range(i + 1, depth))
        use_ctx = cs is not None and (even or ctx_after)
        if not use_ctx:
            cs = None
        mx = [mods[i, :bn, j].reshape(bn, 1, d) for j in range(n_mod)]
        mc = [mods[i, bn:bn + 1, j].reshape(1, 1, d) for j in range(n_mod)]
        last_layer = i == depth - 1

        xs = ffn(xs, i, 0, 0, mx, seq)
        if cs is not None:
            cs = ffn(cs, i, 0, 0, mc, m_c)

        if even:
            e_idx = i // 2
            if ctx_after:
                raise NotImplementedError("context output of an even layer is only needed for depth > 2")
            xs = _ab_mixer(xs, cs, mx, mc, norm_g[i, 1], bn, seq, ctx_len, ab_w_in[e_idx], ab_w_out[e_idx],
                           gm_v_gain[e_idx], gm_ws[e_idx], gm_b[e_idx], rw_mu[e_idx], rw_w0[e_idx],
                           rw_w2[e_idx], rw_a0[e_idx], rw_a2[e_idx], rw_g2[e_idx], rw_k_k[e_idx],
                           rw_k_a[e_idx], rw_r_k[e_idx], rw_ln_g[e_idx], rw_ln_b[e_idx])
        else:
            o_idx = i // 2
            if ctx_after:
                raise NotImplementedError("context output of an odd layer is only needed for depth > 2")
            ups = [(sc_in, (o_idx,), k * d) for k in range(3)]
            xs = _glu_block(xs, norm_g[i, 1], mx[3], mx[4], mx[5], ups, (sc_out, (o_idx,)), kind="conv",
                            coef=1.0, conv_w=(sc_conv, (o_idx,)), rows_per_mod=seq, tm=512)
        cs = None

        xs = ffn(xs, i, 1, 2, mx, seq, final=final_g if last_layer else None)
    return xs.reshape(bn, seq, d)


def _ab_mixer(xs, cs, mx, mc, g, bn, seq, ctx_len, w_in, w_out, v_gain, ws, b_s, mu, w0, w2, a0, a2, g2,
              k_k, k_a, r_k, ln_g, ln_b):
    m_x, d = xs.shape
    m_c = cs.shape[0]
    bw = k_k.shape[0]
    aw = v_gain.shape[0]
    a_cols = 2 * aw
    heads = bw // HEAD_DIM
    wl, al, gl = w2.shape[1], a2.shape[1], g2.shape[0]
    b_cols = w_in.shape[1] - a_cols
    pw = _round_up(b_cols, 512)
    gp = pw - (3 * bw + 2 * wl + 2 * al)

    w_in_bf = w_in.astype(BF16)
    w_in_a = w_in_bf[:, :a_cols]
    w_in_b = jnp.pad(w_in_bf[:, a_cols:], ((0, 0), (0, pw - b_cols)))

    mu_p = jnp.pad(mu, ((0, 0), (0, pw - b_cols)))
    zeros_w = jnp.zeros((wl, bw), F32)
    zeros_a = jnp.zeros((al, bw), F32)
    w2cat = jnp.concatenate([jnp.concatenate([w2[0], zeros_w], 1), jnp.concatenate([zeros_w, w2[1]], 1)], 0)
    a2cat = jnp.concatenate([jnp.concatenate([a2[0], zeros_a], 1), jnp.concatenate([zeros_a, a2[1]], 1)], 0)
    g2p = jnp.pad(g2, ((0, gp - gl), (0, 0)))
    head_id = jnp.arange(min(bw, MXU_WIDTH)) // HEAD_DIM
    e = (head_id[:, None] == head_id[None, :]).astype(BF16)
    consts = (mu_p, w0.reshape(1, 2 * bw), w2cat.astype(BF16), a0.reshape(1, 2 * bw), a2cat.astype(BF16),
              g2p.astype(BF16), k_k.reshape(1, bw), k_a.reshape(1, bw), r_k.reshape(1, bw), e)

    b_full = jnp.repeat(b_s.T, aw // ws.shape[0], axis=1)
    ya = _gmlp_branch(xs, g, mx[3], mx[4], w_in_a, v_gain, ws.astype(BF16), b_full, rows_per_mod=seq)

    fx = _rwkv_front(xs, g, mx[3], mx[4], w_in_b, seq, consts, rows_per_mod=seq)
    fc = _rwkv_front(cs, g, mc[3], mc[4], w_in_b, ctx_len, consts, rows_per_mod=m_c)
    r_x, v_x, kk_x, g_x, bonus_x, lw_x, kka_x, kd_x = fx
    r_c, v_c, kk_c, _, _, lw_c, kka_c, kd_c = fc
    s_zero = jnp.zeros((bn, bw // LANE, LANE, LANE), F32)
    ys = []
    for reverse in (False, True):
        _, s_ctx = _chunk_scan(r_c, v_c, kk_c, lw_c, kka_c, kd_c, s_zero, bn=bn, reverse=reverse)
        y_dir, _ = _chunk_scan(r_x, v_x, kk_x, lw_x, kka_x, kd_x, s_ctx, bn=bn, reverse=reverse)
        ys.append(y_dir)

    w_out_bf = w_out.astype(BF16)
    return _mix_out(xs, mx[5], ya, ys[0], ys[1], bonus_x, g_x, ln_g.reshape(1, bw), ln_b.reshape(1, bw), e,
                    w_out_bf[:aw], w_out_bf[aw:], rows_per_mod=seq)
```

```python
import functools
import math

import jax
import jax.numpy as jnp
from jax import lax
from jax.experimental import pallas as pl
from jax.experimental.pallas import tpu as pltpu

F32 = jnp.float32
BF16 = jnp.bfloat16

GRID_W = 64
CHUNK = 128
HEAD_DIM = 64
RMS_EPS = 1e-6
LN_EPS = 1e-5
GN_EPS = 64e-5
KK_EPS = 1e-12
W_DECAY_SCALE = math.exp(-0.5)

LANE = 128
SUBLANE = 8
MXU_WIDTH = 256
VMEM_LIMIT = 58 * 1024 * 1024


def _cparams(sem):
    return pltpu.CompilerParams(dimension_semantics=sem, vmem_limit_bytes=VMEM_LIMIT)


def _round_up(n, m):
    return (n + m - 1) // m * m


def _dot(a, b):
    return jnp.dot(a, b, preferred_element_type=F32)


def _dot2(a, b):
    hi = a.astype(BF16)
    lo = (a - hi.astype(F32)).astype(BF16)
    return _dot(hi, b) + _dot(lo, b)


def _head_sums(x, e):
    blk = e.shape[0]
    parts = [_dot2(x[:, j:j + blk], e) for j in range(0, x.shape[1], blk)]
    return parts[0] if len(parts) == 1 else jnp.concatenate(parts, axis=1)


ROW_BLOCK = 16
ROW_UNROLL = 8
CAST_COLS = 1024
EPILOGUE_ROWS = 128


def _prenorm_rows(x_ref, g, shift, scale, h_ref, *, src_row=0, dst_row=0, n_rows=None):
    n_rows = x_ref.shape[0] if n_rows is None else n_rows
    gs = g * (1.0 + scale)

    def block(i, carry):
        off = pl.multiple_of(i * ROW_BLOCK, ROW_BLOCK)
        x = x_ref[pl.ds(src_row + off, ROW_BLOCK), :]
        ms = jnp.mean(x * x, axis=-1, keepdims=True)
        h_ref[pl.ds(dst_row + off, ROW_BLOCK), :] = (x * lax.rsqrt(ms + RMS_EPS) * gs + shift).astype(h_ref.dtype)
        return carry

    n_blocks = n_rows // ROW_BLOCK
    lax.fori_loop(0, n_blocks, block, 0, unroll=math.gcd(n_blocks, ROW_UNROLL))


def _mod_body(c_ref, w_ref, b_ref, o_ref):
    s = jax.nn.silu(c_ref[...]).astype(BF16)
    o_ref[...] = _dot(s, w_ref[...].astype(BF16)) + b_ref[...]


def _modulation(c_rows, ada_w, ada_b):
    depth, d, n = ada_w.shape
    tn = 1024 if n % 1024 == 0 else 512
    rows = c_rows.shape[0]
    return pl.pallas_call(
        _mod_body,
        grid=(depth, n // tn),
        in_specs=[
            pl.BlockSpec((rows, d), lambda l, j: (0, 0)),
            pl.BlockSpec((None, d, tn), lambda l, j: (l, 0, j)),
            pl.BlockSpec((None, 1, tn), lambda l, j: (l, 0, j)),
        ],
        out_specs=pl.BlockSpec((None, rows, tn), lambda l, j: (l, 0, j)),
        out_shape=jax.ShapeDtypeStruct((depth, rows, n), F32),
        compiler_params=_cparams(("arbitrary", "arbitrary")),
    )(c_rows, ada_w, ada_b.reshape(depth, 1, n))


def _glu_body(*refs, n_up, kind, coef, final, nf, n_cast):
    x_ref, g_ref, sh_ref, sc_ref, gt_ref = refs[:5]
    up_refs = refs[5:5 + n_up]
    down_ref = refs[5 + n_up]
    idx = 6 + n_up
    cw_ref = fg_ref = None
    if kind == "conv":
        cw_ref = refs[idx]
        idx += 1
    if final:
        fg_ref = refs[idx]
        idx += 1
    cast_in = refs[idx:idx + n_cast]
    idx += n_cast
    o_ref = refs[idx]
    cast_out = refs[idx + 1:idx + 1 + n_cast]
    h_ref = refs[idx + 1 + n_cast]
    f = pl.program_id(1)

    for ci, co in zip(cast_in, cast_out):
        co[...] = ci[...].astype(co.dtype)

    @pl.when(f == 0)
    def _():
        _prenorm_rows(x_ref, g_ref[...], sh_ref[...], sc_ref[...], h_ref)
        o_ref[...] = jnp.zeros_like(o_ref)

    h = h_ref[...]
    ups = [_dot(h, u[...]) for u in up_refs]
    if kind == "swiglu":
        mid = jax.nn.silu(ups[0]) * ups[1]
    else:
        z = ups[1] * ups[2]
        rows = z.shape[0]
        col = lax.broadcasted_iota(jnp.int32, z.shape, 0) % GRID_W
        zp = jnp.where(col == 0, 0.0, pltpu.roll(z, 1, axis=0))
        zn = jnp.where(col == GRID_W - 1, 0.0, pltpu.roll(z, rows - 1, axis=0))
        cw = cw_ref[...]
        mid = ups[0] * (cw[0:1] * zp + cw[1:2] * z + cw[2:3] * zn)
    o_ref[...] += _dot(mid.astype(BF16), down_ref[...])

    @pl.when(f == nf - 1)
    def _():
        cg = coef * gt_ref[...]
        tm = o_ref.shape[0]
        step = min(tm, EPILOGUE_ROWS)
        for r0 in range(0, tm, step):
            rows = slice(r0, r0 + step)
            res = x_ref[rows, :] + cg * o_ref[rows, :]
            if final:
                ms = jnp.mean(res * res, axis=-1, keepdims=True)
                res = res * lax.rsqrt(ms + RMS_EPS) * fg_ref[...]
            o_ref[rows, :] = res


def _cast_plan(shape, steps):
    n = math.prod(shape)
    if n % (steps * CAST_COLS):
        return None
    rows = n // (steps * CAST_COLS)
    return rows if rows % ROW_BLOCK == 0 else None


def _glu_block(x, g, shift, scale, gate, ups, down, *, kind, coef, conv_w=None, final_g=None,
               rows_per_mod, tm=1024, tf=512, cast=()):
    m, d = x.shape
    down_w, down_lead = down
    fdim = down_w.shape[-2]
    tm = min(tm, m, rows_per_mod)
    nf = fdim // tf
    tpb = rows_per_mod // tm
    mod_spec = pl.BlockSpec((None, 1, d), lambda i, f: (i // tpb, 0, 0))
    in_specs = [
        pl.BlockSpec((tm, d), lambda i, f: (i, 0)),
        pl.BlockSpec((1, d), lambda i, f: (0, 0)),
        mod_spec, mod_spec, mod_spec,
    ]
    args = [x, g.reshape(1, d), shift, scale, gate]
    for w, lead, col0 in ups:
        blk0 = col0 // tf
        in_specs.append(pl.BlockSpec((None,) * len(lead) + (d, tf),
                                     lambda i, f, lead=lead, blk0=blk0: lead + (0, blk0 + f)))
        args.append(w)
    in_specs.append(pl.BlockSpec((None,) * len(down_lead) + (tf, d), lambda i, f: down_lead + (f, 0)))
    args.append(down_w)
    if kind == "conv":
        cw, cw_lead = conv_w
        in_specs.append(pl.BlockSpec((None,) * len(cw_lead) + (3, tf), lambda i, f: cw_lead + (0, f)))
        args.append(cw)
    if final_g is not None:
        in_specs.append(pl.BlockSpec((1, d), lambda i, f: (0, 0)))
        args.append(final_g.reshape(1, d))
    steps = (m // tm) * nf
    out_specs = [pl.BlockSpec((tm, d), lambda i, f: (i, 0))]
    out_shape = [jax.ShapeDtypeStruct((m, d), F32)]
    for w in cast:
        rows = _cast_plan(w.shape, steps)
        blk = pl.BlockSpec((rows, CAST_COLS), lambda i, f: (i * nf + f, 0))
        in_specs.append(blk)
        args.append(w.reshape(-1, CAST_COLS))
        out_specs.append(blk)
        out_shape.append(jax.ShapeDtypeStruct((steps * rows, CAST_COLS), BF16))
    body = functools.partial(_glu_body, n_up=len(ups), kind=kind, coef=coef,
                             final=final_g is not None, nf=nf, n_cast=len(cast))
    res = pl.pallas_call(
        body,
        grid=(m // tm, nf),
        in_specs=in_specs,
        out_specs=out_specs,
        out_shape=out_shape,
        scratch_shapes=[pltpu.VMEM((tm, d), BF16)],
        compiler_params=_cparams(("arbitrary", "arbitrary")),
    )(*args)
    return res[0], [c.reshape(w.shape) for c, w in zip(res[1:], cast)]


def _gmlp_body(x_ref, g_ref, sh_ref, sc_ref, w_ref, gain_ref, ws_ref, b_ref, o_ref, h_ref, p_ref, *,
               aw, groups, n_chunks):
    _prenorm_rows(x_ref, g_ref[...], sh_ref[...], sc_ref[...], h_ref)
    p_ref[...] = _dot(h_ref[...], w_ref[...])
    gd = aw // groups
    for c in range(n_chunks):
        rows = pl.ds(c * CHUNK, CHUNK)
        u = jax.nn.gelu(p_ref[rows, 0:aw])
        v = jax.nn.gelu(p_ref[rows, aw:2 * aw])
        vm = jnp.mean(v, axis=-1, keepdims=True)
        vc = v - vm
        vv = jnp.mean(vc * vc, axis=-1, keepdims=True)
        vn = (vc * lax.rsqrt(vv + LN_EPS) * gain_ref[...]).astype(BF16)
        for gi in range(groups):
            cols = slice(gi * gd, (gi + 1) * gd)
            vs = _dot(ws_ref[gi], vn[:, cols]) + b_ref[:, cols]
            o_ref[rows, cols] = (u[:, cols] * vs).astype(o_ref.dtype)


def _gmlp_branch(x, g, shift, scale, w_a, gain, ws, b_full, *, rows_per_mod, tm=512):
    m, d = x.shape
    two_aw = w_a.shape[1]
    aw = two_aw // 2
    groups = ws.shape[0]
    tm = min(tm, m, rows_per_mod)
    tpb = rows_per_mod // tm
    mod_spec = pl.BlockSpec((None, 1, d), lambda i: (i // tpb, 0, 0))
    body = functools.partial(_gmlp_body, aw=aw, groups=groups, n_chunks=tm // CHUNK)
    return pl.pallas_call(
        body,
        grid=(m // tm,),
        in_specs=[
            pl.BlockSpec((tm, d), lambda i: (i, 0)),
            pl.BlockSpec((1, d), lambda i: (0, 0)),
            mod_spec, mod_spec,
            pl.BlockSpec((d, two_aw), lambda i: (0, 0), pipeline_mode=pl.Buffered(1)),
            pl.BlockSpec((1, aw), lambda i: (0, 0)),
            pl.BlockSpec((groups, CHUNK, CHUNK), lambda i: (0, 0, 0)),
            pl.BlockSpec((CHUNK, aw), lambda i: (0, 0)),
        ],
        out_specs=pl.BlockSpec((tm, aw), lambda i: (i, 0)),
        out_shape=jax.ShapeDtypeStruct((m, aw), BF16),
        scratch_shapes=[pltpu.VMEM((tm, d), BF16), pltpu.VMEM((tm, two_aw), F32)],
        compiler_params=_cparams(("arbitrary",)),
    )(x, g.reshape(1, d), shift, scale, w_a, gain.reshape(1, aw), ws, b_full)


def _prep_body(x_ref, xp_ref, xn_ref, g_ref, sh_ref, sc_ref, w_ref, mu_ref, w0_ref, w2_ref, a0_ref, a2_ref,
               g2_ref, kk_ref, ka_ref, rk_ref, e_ref, r_o, v_o, kk_o, g_o, bonus_o, lw_o, kka_o, kd_o,
               h_ref, p_ref, *, bw, wl2, al2, seq, tm):
    i = pl.program_id(0)
    g, sh, sc = g_ref[...], sh_ref[...], sc_ref[...]
    _prenorm_rows(x_ref, g, sh, sc, h_ref)
    _prenorm_rows(xp_ref, g, sh, sc, h_ref, dst_row=tm)
    _prenorm_rows(xn_ref, g, sh, sc, h_ref, dst_row=tm + ROW_BLOCK)
    p_ref[...] = _dot(h_ref[...], w_ref[...])

    p = p_ref[0:tm, :]
    row = lax.broadcasted_iota(jnp.int32, p.shape, 0)
    first = (i * tm) % seq == 0
    last = ((i + 1) * tm) % seq == 0
    prev_row = jnp.where(first, 0.0, p_ref[tm + ROW_BLOCK - 1:tm + ROW_BLOCK, :])
    next_row = jnp.where(last, 0.0, p_ref[tm + ROW_BLOCK:tm + ROW_BLOCK + 1, :])
    prev = jnp.where(row == 0, prev_row, pltpu.roll(p, 1, axis=0))
    nxt = jnp.where(row == tm - 1, next_row, pltpu.roll(p, tm - 1, axis=0))
    mu = mu_ref[...]
    p = p + mu[0:1] * (prev - p) + mu[1:2] * (nxt - p)

    r = p[:, 0:bw]
    k = p[:, bw:2 * bw]
    v = p[:, 2 * bw:3 * bw]
    o = 3 * bw
    wd = jnp.tanh(p[:, o:o + wl2]).astype(BF16)
    ad = p[:, o + wl2:o + wl2 + al2].astype(BF16)
    gdn = jax.nn.sigmoid(p[:, o + wl2 + al2:]).astype(BF16)

    lw = -W_DECAY_SCALE * jax.nn.sigmoid(w0_ref[...] + _dot(wd, w2_ref[...]))
    a = jax.nn.sigmoid(a0_ref[...] + _dot(ad, a2_ref[...]))
    gate = _dot(gdn, g2_ref[...])

    e = e_ref[...]
    kq = k * kk_ref[...]
    kk = kq * lax.rsqrt(_head_sums(kq * kq, e) + KK_EPS)
    ka = ka_ref[...]
    kd0 = k * (1.0 + (a[:, 0:bw] - 1.0) * ka)
    kd1 = k * (1.0 + (a[:, bw:] - 1.0) * ka)
    kb = 0.5 * (kd0 + kd1)
    bonus = _head_sums(r * kb * rk_ref[...], e) * v

    r_o[...] = r
    v_o[...] = v
    kk_o[...] = kk
    g_o[...] = gate
    bonus_o[...] = bonus
    lw_o[...] = lw
    kka_o[:, 0:bw] = kk * a[:, 0:bw]
    kka_o[:, bw:] = kk * a[:, bw:]
    kd_o[:, 0:bw] = kd0
    kd_o[:, bw:] = kd1


def _rwkv_front(x, g, shift, scale, w_b, seq, consts, *, rows_per_mod, tm=256):
    m, d = x.shape
    pw = w_b.shape[1]
    (mu, w0, w2cat, a0, a2cat, g2p, k_k, k_a, r_k, e) = consts
    bw = k_k.shape[1]
    wl2, al2 = w2cat.shape[0], a2cat.shape[0]
    tm = min(tm, seq)
    tpb = rows_per_mod // tm
    nblk = tm // ROW_BLOCK
    last_blk = m // ROW_BLOCK - 1
    full = lambda arr: pl.BlockSpec(arr.shape, lambda i: (0,) * arr.ndim)
    mod_spec = pl.BlockSpec((None, 1, d), lambda i: (i // tpb, 0, 0))
    body = functools.partial(_prep_body, bw=bw, wl2=wl2, al2=al2, seq=seq, tm=tm)
    one = jax.ShapeDtypeStruct((m, bw), F32)
    two = jax.ShapeDtypeStruct((m, 2 * bw), F32)
    o1 = pl.BlockSpec((tm, bw), lambda i: (i, 0))
    o2 = pl.BlockSpec((tm, 2 * bw), lambda i: (i, 0))
    return pl.pallas_call(
        body,
        grid=(m // tm,),
        in_specs=[
            pl.BlockSpec((tm, d), lambda i: (i, 0)),
            pl.BlockSpec((ROW_BLOCK, d), lambda i: (jnp.maximum(i * nblk - 1, 0), 0)),
            pl.BlockSpec((ROW_BLOCK, d), lambda i: (jnp.minimum((i + 1) * nblk, last_blk), 0)),
            pl.BlockSpec((1, d), lambda i: (0, 0)),
            mod_spec, mod_spec,
            pl.BlockSpec((d, pw), lambda i: (0, 0), pipeline_mode=pl.Buffered(1)),
            full(mu), full(w0), full(w2cat), full(a0), full(a2cat), full(g2p), full(k_k), full(k_a),
            full(r_k), full(e),
        ],
        out_specs=[o1, o1, o1, o1, o1, o2, o2, o2],
        out_shape=[one, one, one, one, one, two, two, two],
        scratch_shapes=[pltpu.VMEM((tm + 2 * ROW_BLOCK, d), BF16), pltpu.VMEM((tm + 2 * ROW_BLOCK, pw), F32)],
        compiler_params=_cparams(("arbitrary",)),
    )(x, x, x, g.reshape(1, d), shift, scale, w_b, mu, w0, w2cat, a0, a2cat, g2p, k_k, k_a, r_k, e)


SCAN_CHUNK = 64
INV_BLOCK = 16


def _dot_nt(a, b):
    return lax.dot_general(a, b, (((1,), (1,)), ((), ())), preferred_element_type=F32)


def _dot_tn(a, b):
    return lax.dot_general(a, b, (((0,), (0,)), ((), ())), preferred_element_type=F32)


def _pair_blockdiag(x, lane_lo):
    return jnp.concatenate([jnp.where(lane_lo, x, 0.0), jnp.where(lane_lo, 0.0, x)], axis=0).astype(BF16)


def _chunk_scan_body(r_ref, v_ref, kk_ref, lw_ref, kka_ref, kd_ref, s0_ref, y_ref, sout_ref,
                     s_ref, qw_ref, u0_ref, arb_ref, pe_ref, vb_ref, elt_ref, *, reverse, nsub, pairs, group):
    c = SCAN_CHUNK
    step = pl.program_id(1)
    bf = lambda t: t.astype(BF16)

    @pl.when(step == 0)
    def _():
        s_ref[...] = s0_ref[...]

    ri = lax.broadcasted_iota(jnp.int32, (c, c), 0)
    ci = lax.broadcasted_iota(jnp.int32, (c, c), 1)
    tri = ((ri <= ci) if reverse else (ri >= ci)).astype(BF16)
    rp = lax.broadcasted_iota(jnp.int32, (c, LANE), 0)
    cp = lax.broadcasted_iota(jnp.int32, (c, LANE), 1) % c
    incl_p = (rp <= cp) if reverse else (rp >= cp)
    strict_p = (rp < cp) if reverse else (rp > cp)
    r2 = lax.broadcasted_iota(jnp.int32, (2 * c, LANE), 0)
    c2 = lax.broadcasted_iota(jnp.int32, (2 * c, LANE), 1) % c
    t2 = r2 % c
    mask_kr = ((t2 < c2) if reverse else (t2 > c2)) | ((r2 >= c) & (t2 == c2))
    blk_p = (rp // INV_BLOCK) == (cp // INV_BLOCK)
    eye_p = (rp == cp).astype(F32)
    lane_lo = lax.broadcasted_iota(jnp.int32, (c, LANE), 1) < HEAD_DIM
    bi = lax.broadcasted_iota(jnp.int32, (LANE, LANE), 0) // HEAD_DIM
    bj = lax.broadcasted_iota(jnp.int32, (LANE, LANE), 1) // HEAD_DIM
    same_head = bi == bj
    bd = lambda t: _pair_blockdiag(t, lane_lo)
    prange = range(pairs)
    lanes = [pl.ds(p * LANE, LANE) for p in prange]

    def phase_a(chunks):
        streams = [(j, p) for j in chunks for p in prange]
        sr = range(len(streams))
        rows = [pl.ds(j * c, c) for j, _ in streams]
        cols = [lanes[p] for _, p in streams]
        lw = [lw_ref[rows[s], cols[s]] for s in sr]
        lc = []
        for s in sr:
            hi = bf(lw[s])
            lc.append(_dot(tri, hi) + _dot(tri, bf(lw[s] - hi.astype(F32))))
        yield
        qq, qk_bd, qr, pk_bd, pb_bd, v_bd = [], [], [], [], [], []
        for s, (j, p) in enumerate(streams):
            ltot = lc[s][0:1] if reverse else lc[s][c - 1:c]
            e_neg = jnp.exp(-lc[s])
            e_end = jnp.exp(ltot - lc[s])
            kd = kd_ref[rows[s], cols[s]]
            kka = kka_ref[rows[s], cols[s]]
            v = v_ref[rows[s], cols[s]]
            qk = kk_ref[rows[s], cols[s]] * jnp.exp(lc[s] - lw[s])
            qr.append(r_ref[rows[s], cols[s]] * jnp.exp(lc[s]))
            qq.append(bf(jnp.concatenate([qk, qr[s]], axis=0)))
            qk_bd.append(bd(qk))
            pk_bd.append(bd(kd * e_neg))
            pb_bd.append(bd(kka * e_neg))
            v_bd.append(bd(v))
            pe_ref[j, p] = bf(jnp.concatenate([kd * e_end, -(kka * e_end)], axis=0))
            vb_ref[j, p] = bf(v)
            elt_ref[j, p] = jnp.broadcast_to(jnp.exp(ltot), (SUBLANE, LANE))
        g = [_dot_nt(qq[s], jnp.concatenate([pk_bd[s], pb_bd[s]], axis=0)) for s in sr]
        yield
        a_k = [bf(jnp.where(mask_kr, g[s][:, :LANE], 0.0)) for s in sr]
        nmat = [jnp.where(strict_p, g[s][:c, LANE:], 0.0) for s in sr]
        for s, (j, p) in enumerate(streams):
            arb_ref[j, p] = bf(jnp.where(incl_p, g[s][c:, LANE:], 0.0))
        avy = [_dot(a_k[s], v_bd[s]) for s in sr]
        av = [avy[s][:c] for s in sr]
        for s in sr:
            y_ref[rows[s], cols[s]] = avy[s][c:]
        nd = [jnp.where(blk_p, nmat[s], 0.0) for s in sr]
        lo_bd = [bd(nmat[s] - nd[s]) for s in sr]
        dinv = [eye_p - nd[s] for s in sr]
        mpow = [_dot(bf(-nd[s]), bd(-nd[s])) for s in sr]
        yield
        for _ in range(int(math.log2(INV_BLOCK)) - 2):
            both = [_dot(bf(jnp.concatenate([mpow[s], dinv[s]], axis=0)), bd(mpow[s])) for s in sr]
            mpow = [both[s][:c] for s in sr]
            dinv = [dinv[s] + both[s][c:] for s in sr]
            yield
        dinv = [dinv[s] + _dot(bf(dinv[s]), bd(mpow[s])) for s in sr]
        yield
        dinv_b = [bf(dinv[s]) for s in sr]
        x1 = [_dot(dinv_b[s], lo_bd[s]) for s in sr]
        yield
        x1b = [bf(x1[s]) for s in sr]
        acc = [eye_p - x1[s] for s in sr]
        xp = x1
        for k in range(2, SCAN_CHUNK // INV_BLOCK):
            xp = [_dot(x1b[s], bd(xp[s])) for s in sr]
            yield
            acc = [acc[s] + xp[s] if k % 2 == 0 else acc[s] - xp[s] for s in sr]
        tinv = [bf(_dot(bf(acc[s]), bd(dinv[s]))) for s in sr]
        yield
        wt = [_dot(tinv[s], qk_bd[s]) for s in sr]
        for s, (j, p) in enumerate(streams):
            u0_ref[j, p] = _dot(tinv[s], bd(av[s]))
            qw_ref[j, p] = bf(jnp.concatenate([wt[s], qr[s]], axis=0))

    def phase_b(chunks):
        for jj in chunks:
            rows = pl.ds(jj * c, c)
            s = [s_ref[p] for p in prange]
            xs = [_dot_nt(qw_ref[jj, p], bf(s[p])) for p in prange]
            yield
            u = [u0_ref[jj, p] + xs[p][:c] for p in prange]
            au = [_dot(arb_ref[jj, p], bd(u[p])) for p in prange]
            upd = [_dot_tn(jnp.concatenate([vb_ref[jj, p], bf(u[p])], axis=0), pe_ref[jj, p]) for p in prange]
            yield
            for p in prange:
                y_ref[rows, lanes[p]] = y_ref[rows, lanes[p]] + xs[p][c:] - au[p]
                s_ref[p] = s[p] * elt_ref[jj, p][0:1] + jnp.where(same_head, upd[p], 0.0)
            yield

    order = list(range(nsub))[::-1] if reverse else list(range(nsub))
    groups = [order[i:i + group] for i in range(0, nsub, group)]
    pending = None
    for chunks in groups + [None]:
        live = [gen for gen in (phase_a(chunks) if chunks is not None else None, pending) if gen is not None]
        while live:
            for gen in list(live):
                try:
                    next(gen)
                except StopIteration:
                    live.remove(gen)
        pending = phase_b(chunks) if chunks is not None else None

    @pl.when(step == pl.num_programs(1) - 1)
    def _():
        sout_ref[...] = s_ref[...]


def _chunk_scan(r, v, kk, lw, kka, kd, s0, *, bn, reverse, tt=512):
    m, bw = r.shape
    seq = m // bn
    tt = min(tt, seq)
    steps = seq // tt
    pairs = bw // LANE
    d = 1 if reverse else 0

    def row_blk(b, s):
        return b * steps + (steps - 1 - s if reverse else s)

    tok = pl.BlockSpec((tt, bw), lambda b, s: (row_blk(b, s), 0))
    tok_dir = pl.BlockSpec((tt, bw), lambda b, s: (row_blk(b, s), d))
    st = pl.BlockSpec((None, pairs, LANE, LANE), lambda b, s: (b, 0, 0, 0))
    c = SCAN_CHUNK
    nsub = tt // c
    group = 2 if nsub % 2 == 0 else 1
    body = functools.partial(_chunk_scan_body, reverse=reverse, nsub=nsub, pairs=pairs, group=group)
    return pl.pallas_call(
        body,
        grid=(bn, steps),
        in_specs=[tok, tok, tok, tok_dir, tok_dir, tok_dir, st],
        out_specs=[tok, st],
        out_shape=[jax.ShapeDtypeStruct((m, bw), F32), jax.ShapeDtypeStruct(s0.shape, F32)],
        scratch_shapes=[
            pltpu.VMEM((pairs, LANE, LANE), F32),
            pltpu.VMEM((nsub, pairs, 2 * c, LANE), BF16),
            pltpu.VMEM((nsub, pairs, c, LANE), F32),
            pltpu.VMEM((nsub, pairs, c, LANE), BF16),
            pltpu.VMEM((nsub, pairs, 2 * c, LANE), BF16),
            pltpu.VMEM((nsub, pairs, c, LANE), BF16),
            pltpu.VMEM((nsub, pairs, SUBLANE, LANE), F32),
        ],
        compiler_params=_cparams(("arbitrary", "arbitrary")),
    )(r, v, kk, lw, kka, kd, s0)


def _mixout_body(x_ref, gt_ref, ya_ref, yf_ref, yb_ref, bonus_ref, g_ref, lng_ref, lnb_ref, e_ref, wa_ref,
                 wb_ref, o_ref):
    e = e_ref[...]
    inv_n = 1.0 / HEAD_DIM
    y = yf_ref[...] + yb_ref[...]
    yc = y - _head_sums(y, e) * inv_n
    var = _head_sums(yc * yc, e) * inv_n
    yn = yc * lax.rsqrt(var + GN_EPS) * lng_ref[...] + lnb_ref[...]
    yb = ((yn + bonus_ref[...]) * g_ref[...]).astype(BF16)
    ox = _dot(ya_ref[...], wa_ref[...]) + _dot(yb, wb_ref[...])
    o_ref[...] = x_ref[...] + gt_ref[...] * ox


def _mix_out(x, gate, ya, y_fwd, y_bwd, bonus, g, ln_g, ln_b, e, w_out_a, w_out_b, *, rows_per_mod, tm=256):
    m, d = x.shape
    aw, bw = ya.shape[1], y_fwd.shape[1]
    tpb = rows_per_mod // tm
    full = lambda arr: pl.BlockSpec(arr.shape, lambda i: (0,) * arr.ndim)
    tok = lambda width: pl.BlockSpec((tm, width), lambda i: (i, 0))
    return pl.pallas_call(
        _mixout_body,
        grid=(m // tm,),
        in_specs=[
            tok(d), pl.BlockSpec((None, 1, d), lambda i: (i // tpb, 0, 0)),
            tok(aw), tok(bw), tok(bw), tok(bw), tok(bw),
            full(ln_g), full(ln_b), full(e), full(w_out_a), full(w_out_b),
        ],
        out_specs=tok(d),
        out_shape=jax.ShapeDtypeStruct((m, d), F32),
        compiler_params=_cparams(("arbitrary",)),
    )(x, gate, ya, y_fwd, y_bwd, bonus, g, ln_g, ln_b, e, w_out_a, w_out_b)


def kernel(x, c, ctx, c_ctx, ada_w, ada_b, norm_g, ffn_w1, ffn_w3, ffn_w2, ab_w_in, ab_w_out, gm_v_gain,
           gm_ws, gm_b, rw_mu, rw_w0, rw_w2, rw_a0, rw_a2, rw_g2, rw_k_k, rw_k_a, rw_r_k, rw_ln_g, rw_ln_b,
           sc_w_in, sc_conv, sc_w_out, final_g):
    bn, seq, d = x.shape
    ctx_len = ctx.shape[1]
    depth = ada_w.shape[0]
    n_mod = ada_w.shape[2] // d
    m_x, m_c = bn * seq, bn * ctx_len
    xs = x.reshape(m_x, d)
    cs = ctx.reshape(m_c, d)

    rows = _round_up(bn + 1, SUBLANE)
    c_rows = jnp.zeros((rows, d), F32).at[:bn].set(c).at[bn].set(c_ctx)
    mods = _modulation(c_rows, ada_w, ada_b).reshape(depth, rows, n_mod, d)

    sc_in, sc_out = sc_w_in.astype(BF16), sc_w_out.astype(BF16)

    ffn_order = [(i, half) for i in range(depth) for half in (0, 1)]
    ffn_bf16 = {ffn_order[0]: tuple(w[ffn_order[0]].astype(BF16) for w in (ffn_w1, ffn_w3, ffn_w2))}
    ffn_tm, ffn_tf = 1024, 512
    ffn_steps = (m_x // min(ffn_tm, seq)) * (ffn_w2.shape[-2] // ffn_tf)

    def ffn(t, key, sub, mod, rows_per_mod, final=None, prepare=None):
        i = key[0]
        w1, w3, w2 = ffn_bf16[key]
        cast = ()
        if prepare is not None:
            cast = tuple(w[prepare] for w in (ffn_w1, ffn_w3, ffn_w2))
            if any(_cast_plan(w.shape, ffn_steps) is None for w in cast):
                ffn_bf16[prepare] = tuple(w.astype(BF16) for w in cast)
                cast = ()
        out, copies = _glu_block(t, norm_g[i, sub], mod[3 * sub], mod[3 * sub + 1], mod[3 * sub + 2],
                                 [(w1, (), 0), (w3, (), 0)], (w2, ()), kind="swiglu", coef=0.5, final_g=final,
                                 rows_per_mod=rows_per_mod, tm=ffn_tm, tf=ffn_tf if final is None else 256,
                                 cast=cast)
        if cast:
            ffn_bf16[prepare] = tuple(copies)
        return out

    def next_key(key):
        k = ffn_order.index(key) + 1
        return ffn_order[k] if k < len(ffn_order) else None

    for i in range(depth):
        even = i % 2 == 0
        ctx_after = any(j % 2 == 0 for j in range(i + 1, depth))
        use_ctx = cs is not None and (even or ctx_after)
        if not use_ctx:
            cs = None
        mx = [mods[i, :bn, j].reshape(bn, 1, d) for j in range(n_mod)]
        mc = [mods[i, bn:bn + 1, j].reshape(1, 1, d) for j in range(n_mod)]
        last_layer = i == depth - 1

        xs = ffn(xs, (i, 0), 0, mx, seq, prepare=next_key((i, 0)))
        if cs is not None:
            cs = ffn(cs, (i, 0), 0, mc, m_c)

        if even:
            e_idx = i // 2
            if ctx_after:
                raise NotImplementedError("context output of an even layer is only needed for depth > 2")
            xs = _ab_mixer(xs, cs, mx, mc, norm_g[i, 1], bn, seq, ctx_len, ab_w_in[e_idx], ab_w_out[e_idx],
                           gm_v_gain[e_idx], gm_ws[e_idx], gm_b[e_idx], rw_mu[e_idx], rw_w0[e_idx],
                           rw_w2[e_idx], rw_a0[e_idx], rw_a2[e_idx], rw_g2[e_idx], rw_k_k[e_idx],
                           rw_k_a[e_idx], rw_r_k[e_idx], rw_ln_g[e_idx], rw_ln_b[e_idx])
        else:
            o_idx = i // 2
            if ctx_after:
                raise NotImplementedError("context output of an odd layer is only needed for depth > 2")
            ups = [(sc_in, (o_idx,), k * d) for k in range(3)]
            xs, _ = _glu_block(xs, norm_g[i, 1], mx[3], mx[4], mx[5], ups, (sc_out, (o_idx,)), kind="conv",
                               coef=1.0, conv_w=(sc_conv, (o_idx,)), rows_per_mod=seq, tm=512)
        cs = None

        xs = ffn(xs, (i, 1), 2, mx, seq, final=final_g if last_layer else None, prepare=next_key((i, 1)))
    return xs.reshape(bn, seq, d)


def _ab_mixer(xs, cs, mx, mc, g, bn, seq, ctx_len, w_in, w_out, v_gain, ws, b_s, mu, w0, w2, a0, a2, g2,
              k_k, k_a, r_k, ln_g, ln_b):
    m_x, d = xs.shape
    m_c = cs.shape[0]
    bw = k_k.shape[0]
    aw = v_gain.shape[0]
    a_cols = 2 * aw
    heads = bw // HEAD_DIM
    wl, al, gl = w2.shape[1], a2.shape[1], g2.shape[0]
    b_cols = w_in.shape[1] - a_cols
    pw = _round_up(b_cols, 512)
    gp = pw - (3 * bw + 2 * wl + 2 * al)

    w_in_bf = w_in.astype(BF16)
    w_in_a = w_in_bf[:, :a_cols]
    w_in_b = jnp.pad(w_in_bf[:, a_cols:], ((0, 0), (0, pw - b_cols)))

    mu_p = jnp.pad(mu, ((0, 0), (0, pw - b_cols)))
    zeros_w = jnp.zeros((wl, bw), F32)
    zeros_a = jnp.zeros((al, bw), F32)
    w2cat = jnp.concatenate([jnp.concatenate([w2[0], zeros_w], 1), jnp.concatenate([zeros_w, w2[1]], 1)], 0)
    a2cat = jnp.concatenate([jnp.concatenate([a2[0], zeros_a], 1), jnp.concatenate([zeros_a, a2[1]], 1)], 0)
    g2p = jnp.pad(g2, ((0, gp - gl), (0, 0)))
    head_id = jnp.arange(min(bw, MXU_WIDTH)) // HEAD_DIM
    e = (head_id[:, None] == head_id[None, :]).astype(BF16)
    consts = (mu_p, w0.reshape(1, 2 * bw), w2cat.astype(BF16), a0.reshape(1, 2 * bw), a2cat.astype(BF16),
              g2p.astype(BF16), k_k.reshape(1, bw), k_a.reshape(1, bw), r_k.reshape(1, bw), e)

    b_full = jnp.repeat(b_s.T, aw // ws.shape[0], axis=1)
    ya = _gmlp_branch(xs, g, mx[3], mx[4], w_in_a, v_gain, ws.astype(BF16), b_full, rows_per_mod=seq)

    fx = _rwkv_front(xs, g, mx[3], mx[4], w_in_b, seq, consts, rows_per_mod=seq)
    fc = _rwkv_front(cs, g, mc[3], mc[4], w_in_b, ctx_len, consts, rows_per_mod=m_c)
    r_x, v_x, kk_x, g_x, bonus_x, lw_x, kka_x, kd_x = fx
    r_c, v_c, kk_c, _, _, lw_c, kka_c, kd_c = fc
    s_zero = jnp.zeros((bn, bw // LANE, LANE, LANE), F32)
    ys = []
    for reverse in (False, True):
        _, s_ctx = _chunk_scan(r_c, v_c, kk_c, lw_c, kka_c, kd_c, s_zero, bn=bn, reverse=reverse)
        y_dir, _ = _chunk_scan(r_x, v_x, kk_x, lw_x, kka_x, kd_x, s_ctx, bn=bn, reverse=reverse)
        ys.append(y_dir)

    w_out_bf = w_out.astype(BF16)
    return _mix_out(xs, mx[5], ya, ys[0], ys[1], bonus_x, g_x, ln_g.reshape(1, bw), ln_b.reshape(1, bw), e,
                    w_out_bf[:aw], w_out_bf[aw:], rows_per_mod=seq)
```

```python
import functools
import math

import jax
import jax.numpy as jnp
from jax import lax
from jax.experimental import pallas as pl
from jax.experimental.pallas import tpu as pltpu

F32 = jnp.float32
BF16 = jnp.bfloat16

GRID_W = 64
CHUNK = 128
HEAD_DIM = 64
RMS_EPS = 1e-6
LN_EPS = 1e-5
GN_EPS = 64e-5
KK_EPS = 1e-12
W_DECAY_SCALE = math.exp(-0.5)

LANE = 128
SUBLANE = 8
MXU_WIDTH = 256
VMEM_LIMIT = 58 * 1024 * 1024


def _cparams(sem):
    return pltpu.CompilerParams(dimension_semantics=sem, vmem_limit_bytes=VMEM_LIMIT)


def _round_up(n, m):
    return (n + m - 1) // m * m


def _dot(a, b):
    return jnp.dot(a, b, preferred_element_type=F32)


def _dot2(a, b):
    hi = a.astype(BF16)
    lo = (a - hi.astype(F32)).astype(BF16)
    return _dot(hi, b) + _dot(lo, b)


def _head_sums(x, e):
    blk = e.shape[0]
    parts = [_dot2(x[:, j:j + blk], e) for j in range(0, x.shape[1], blk)]
    return parts[0] if len(parts) == 1 else jnp.concatenate(parts, axis=1)


ROW_BLOCK = 16
ROW_UNROLL = 8
EPILOGUE_ROWS = 128


def _prenorm_rows(x_ref, g, shift, scale, h_ref, *, src_row=0, dst_row=0, n_rows=None):
    n_rows = x_ref.shape[0] if n_rows is None else n_rows
    gs = g * (1.0 + scale)

    def block(i, carry):
        off = pl.multiple_of(i * ROW_BLOCK, ROW_BLOCK)
        x = x_ref[pl.ds(src_row + off, ROW_BLOCK), :]
        ms = jnp.mean(x * x, axis=-1, keepdims=True)
        h_ref[pl.ds(dst_row + off, ROW_BLOCK), :] = (x * lax.rsqrt(ms + RMS_EPS) * gs + shift).astype(h_ref.dtype)
        return carry

    n_blocks = n_rows // ROW_BLOCK
    lax.fori_loop(0, n_blocks, block, 0, unroll=math.gcd(n_blocks, ROW_UNROLL))


def _mod_body(c_ref, w_ref, b_ref, o_ref):
    s = jax.nn.silu(c_ref[...]).astype(BF16)
    o_ref[...] = _dot(s, w_ref[...].astype(BF16)) + b_ref[...]


def _modulation(c_rows, ada_w, ada_b):
    depth, d, n = ada_w.shape
    tn = 1024 if n % 1024 == 0 else 512
    rows = c_rows.shape[0]
    return pl.pallas_call(
        _mod_body,
        grid=(depth, n // tn),
        in_specs=[
            pl.BlockSpec((rows, d), lambda l, j: (0, 0)),
            pl.BlockSpec((None, d, tn), lambda l, j: (l, 0, j)),
            pl.BlockSpec((None, 1, tn), lambda l, j: (l, 0, j)),
        ],
        out_specs=pl.BlockSpec((None, rows, tn), lambda l, j: (l, 0, j)),
        out_shape=jax.ShapeDtypeStruct((depth, rows, n), F32),
        compiler_params=_cparams(("arbitrary", "arbitrary")),
    )(c_rows, ada_w, ada_b.reshape(depth, 1, n))


def _glu_body(*refs, n_up, kind, coef, final, nf, n_cast):
    x_ref, g_ref, sh_ref, sc_ref, gt_ref = refs[:5]
    up_refs = refs[5:5 + n_up]
    down_ref = refs[5 + n_up]
    idx = 6 + n_up
    cw_ref = fg_ref = None
    if kind == "conv":
        cw_ref = refs[idx]
        idx += 1
    if final:
        fg_ref = refs[idx]
        idx += 1
    cast_in = refs[idx:idx + n_cast]
    idx += n_cast
    o_ref = refs[idx]
    cast_out = refs[idx + 1:idx + 1 + n_cast]
    h_ref = refs[idx + 1 + n_cast]
    f = pl.program_id(1)

    for ci, co in zip(cast_in, cast_out):
        co[...] = ci[...].astype(co.dtype)

    @pl.when(f == 0)
    def _():
        _prenorm_rows(x_ref, g_ref[...], sh_ref[...], sc_ref[...], h_ref)
        o_ref[...] = jnp.zeros_like(o_ref)

    h = h_ref[...]
    ups = [_dot(h, u[...]) for u in up_refs]
    if kind == "swiglu":
        mid = jax.nn.silu(ups[0]) * ups[1]
    else:
        z = ups[1] * ups[2]
        rows = z.shape[0]
        col = lax.broadcasted_iota(jnp.int32, z.shape, 0) % GRID_W
        zp = jnp.where(col == 0, 0.0, pltpu.roll(z, 1, axis=0))
        zn = jnp.where(col == GRID_W - 1, 0.0, pltpu.roll(z, rows - 1, axis=0))
        cw = cw_ref[...]
        mid = ups[0] * (cw[0:1] * zp + cw[1:2] * z + cw[2:3] * zn)
    o_ref[...] += _dot(mid.astype(BF16), down_ref[...])

    @pl.when(f == nf - 1)
    def _():
        cg = coef * gt_ref[...]
        tm = o_ref.shape[0]
        step = min(tm, EPILOGUE_ROWS)
        for r0 in range(0, tm, step):
            rows = slice(r0, r0 + step)
            res = x_ref[rows, :] + cg * o_ref[rows, :]
            if final:
                ms = jnp.mean(res * res, axis=-1, keepdims=True)
                res = res * lax.rsqrt(ms + RMS_EPS) * fg_ref[...]
            o_ref[rows, :] = res


def _cast_plan(mat, d, n_tiles, tf):
    if d % n_tiles or (d // n_tiles) % LANE:
        return None
    if mat[0] == d:
        return (d // n_tiles, tf), (lambda i, f: (i, f))
    return (tf, d // n_tiles), (lambda i, f: (f, i))


def _glu_block(x, g, shift, scale, gate, ups, down, *, kind, coef, conv_w=None, final_g=None,
               rows_per_mod, tm=1024, tf=512, cast=()):
    m, d = x.shape
    down_w, down_lead = down
    fdim = down_w.shape[-2]
    tm = min(tm, m, rows_per_mod)
    nf = fdim // tf
    tpb = rows_per_mod // tm
    mod_spec = pl.BlockSpec((None, 1, d), lambda i, f: (i // tpb, 0, 0))
    in_specs = [
        pl.BlockSpec((tm, d), lambda i, f: (i, 0)),
        pl.BlockSpec((1, d), lambda i, f: (0, 0)),
        mod_spec, mod_spec, mod_spec,
    ]
    args = [x, g.reshape(1, d), shift, scale, gate]
    for w, lead, col0 in ups:
        blk0 = col0 // tf
        in_specs.append(pl.BlockSpec((None,) * len(lead) + (d, tf),
                                     lambda i, f, lead=lead, blk0=blk0: lead + (0, blk0 + f)))
        args.append(w)
    in_specs.append(pl.BlockSpec((None,) * len(down_lead) + (tf, d), lambda i, f: down_lead + (f, 0)))
    args.append(down_w)
    if kind == "conv":
        cw, cw_lead = conv_w
        in_specs.append(pl.BlockSpec((None,) * len(cw_lead) + (3, tf), lambda i, f: cw_lead + (0, f)))
        args.append(cw)
    if final_g is not None:
        in_specs.append(pl.BlockSpec((1, d), lambda i, f: (0, 0)))
        args.append(final_g.reshape(1, d))
    steps = (m // tm) * nf
    out_specs = [pl.BlockSpec((tm, d), lambda i, f: (i, 0))]
    out_shape = [jax.ShapeDtypeStruct((m, d), F32)]
    for w, lead in cast:
        blk, order = _cast_plan(w.shape[-2:], d, m // tm, tf)
        in_specs.append(pl.BlockSpec((None,) * len(lead) + blk,
                                     lambda i, f, lead=lead, order=order: lead + order(i, f)))
        args.append(w)
        out_specs.append(pl.BlockSpec(blk, order))
        out_shape.append(jax.ShapeDtypeStruct(w.shape[-2:], BF16))
    body = functools.partial(_glu_body, n_up=len(ups), kind=kind, coef=coef,
                             final=final_g is not None, nf=nf, n_cast=len(cast))
    res = pl.pallas_call(
        body,
        grid=(m // tm, nf),
        in_specs=in_specs,
        out_specs=out_specs,
        out_shape=out_shape,
        scratch_shapes=[pltpu.VMEM((tm, d), BF16)],
        compiler_params=_cparams(("arbitrary", "arbitrary")),
    )(*args)
    return res[0], list(res[1:])


def _cast_body(i_ref, o_ref):
    o_ref[...] = i_ref[...].astype(o_ref.dtype)


def _cast_set(w, lead, rows=512):
    n_rows, n_cols = w.shape[-2:]
    rows = math.gcd(rows, n_rows)
    return pl.pallas_call(
        _cast_body,
        grid=(n_rows // rows,),
        in_specs=[pl.BlockSpec((None,) * len(lead) + (rows, n_cols), lambda i: lead + (i, 0))],
        out_specs=pl.BlockSpec((rows, n_cols), lambda i: (i, 0)),
        out_shape=jax.ShapeDtypeStruct((n_rows, n_cols), BF16),
        compiler_params=_cparams(("arbitrary",)),
    )(w)


def _gmlp_body(x_ref, g_ref, sh_ref, sc_ref, w_ref, gain_ref, ws_ref, b_ref, o_ref, h_ref, p_ref, *,
               aw, groups, n_chunks):
    _prenorm_rows(x_ref, g_ref[...], sh_ref[...], sc_ref[...], h_ref)
    p_ref[...] = _dot(h_ref[...], w_ref[...])
    gd = aw // groups
    for c in range(n_chunks):
        rows = pl.ds(c * CHUNK, CHUNK)
        u = jax.nn.gelu(p_ref[rows, 0:aw])
        v = jax.nn.gelu(p_ref[rows, aw:2 * aw])
        vm = jnp.mean(v, axis=-1, keepdims=True)
        vc = v - vm
        vv = jnp.mean(vc * vc, axis=-1, keepdims=True)
        vn = (vc * lax.rsqrt(vv + LN_EPS) * gain_ref[...]).astype(BF16)
        for gi in range(groups):
            cols = slice(gi * gd, (gi + 1) * gd)
            vs = _dot(ws_ref[gi], vn[:, cols]) + b_ref[:, cols]
            o_ref[rows, cols] = (u[:, cols] * vs).astype(o_ref.dtype)


def _gmlp_branch(x, g, shift, scale, w_a, gain, ws, b_full, *, rows_per_mod, tm=512):
    m, d = x.shape
    two_aw = w_a.shape[1]
    aw = two_aw // 2
    groups = ws.shape[0]
    tm = min(tm, m, rows_per_mod)
    tpb = rows_per_mod // tm
    mod_spec = pl.BlockSpec((None, 1, d), lambda i: (i // tpb, 0, 0))
    body = functools.partial(_gmlp_body, aw=aw, groups=groups, n_chunks=tm // CHUNK)
    return pl.pallas_call(
        body,
        grid=(m // tm,),
        in_specs=[
            pl.BlockSpec((tm, d), lambda i: (i, 0)),
            pl.BlockSpec((1, d), lambda i: (0, 0)),
            mod_spec, mod_spec,
            pl.BlockSpec((d, two_aw), lambda i: (0, 0), pipeline_mode=pl.Buffered(1)),
            pl.BlockSpec((1, aw), lambda i: (0, 0)),
            pl.BlockSpec((groups, CHUNK, CHUNK), lambda i: (0, 0, 0)),
            pl.BlockSpec((CHUNK, aw), lambda i: (0, 0)),
        ],
        out_specs=pl.BlockSpec((tm, aw), lambda i: (i, 0)),
        out_shape=jax.ShapeDtypeStruct((m, aw), BF16),
        scratch_shapes=[pltpu.VMEM((tm, d), BF16), pltpu.VMEM((tm, two_aw), F32)],
        compiler_params=_cparams(("arbitrary",)),
    )(x, g.reshape(1, d), shift, scale, w_a, gain.reshape(1, aw), ws, b_full)


def _prep_body(x_ref, xp_ref, xn_ref, g_ref, sh_ref, sc_ref, w_ref, mu_ref, w0_ref, w2_ref, a0_ref, a2_ref,
               g2_ref, kk_ref, ka_ref, rk_ref, e_ref, r_o, v_o, kk_o, g_o, bonus_o, lw_o, kka_o, kd_o,
               h_ref, p_ref, *, bw, wl2, al2, seq, tm):
    i = pl.program_id(0)
    g, sh, sc = g_ref[...], sh_ref[...], sc_ref[...]
    _prenorm_rows(x_ref, g, sh, sc, h_ref)
    _prenorm_rows(xp_ref, g, sh, sc, h_ref, dst_row=tm)
    _prenorm_rows(xn_ref, g, sh, sc, h_ref, dst_row=tm + ROW_BLOCK)
    p_ref[...] = _dot(h_ref[...], w_ref[...])

    p = p_ref[0:tm, :]
    row = lax.broadcasted_iota(jnp.int32, p.shape, 0)
    first = (i * tm) % seq == 0
    last = ((i + 1) * tm) % seq == 0
    prev_row = jnp.where(first, 0.0, p_ref[tm + ROW_BLOCK - 1:tm + ROW_BLOCK, :])
    next_row = jnp.where(last, 0.0, p_ref[tm + ROW_BLOCK:tm + ROW_BLOCK + 1, :])
    prev = jnp.where(row == 0, prev_row, pltpu.roll(p, 1, axis=0))
    nxt = jnp.where(row == tm - 1, next_row, pltpu.roll(p, tm - 1, axis=0))
    mu = mu_ref[...]
    p = p + mu[0:1] * (prev - p) + mu[1:2] * (nxt - p)

    r = p[:, 0:bw]
    k = p[:, bw:2 * bw]
    v = p[:, 2 * bw:3 * bw]
    o = 3 * bw
    wd = jnp.tanh(p[:, o:o + wl2]).astype(BF16)
    ad = p[:, o + wl2:o + wl2 + al2].astype(BF16)
    gdn = jax.nn.sigmoid(p[:, o + wl2 + al2:]).astype(BF16)

    lw = -W_DECAY_SCALE * jax.nn.sigmoid(w0_ref[...] + _dot(wd, w2_ref[...]))
    a = jax.nn.sigmoid(a0_ref[...] + _dot(ad, a2_ref[...]))
    gate = _dot(gdn, g2_ref[...])

    e = e_ref[...]
    kq = k * kk_ref[...]
    kk = kq * lax.rsqrt(_head_sums(kq * kq, e) + KK_EPS)
    ka = ka_ref[...]
    kd0 = k * (1.0 + (a[:, 0:bw] - 1.0) * ka)
    kd1 = k * (1.0 + (a[:, bw:] - 1.0) * ka)
    kb = 0.5 * (kd0 + kd1)
    bonus = _head_sums(r * kb * rk_ref[...], e) * v

    r_o[...] = r
    v_o[...] = v
    kk_o[...] = kk
    g_o[...] = gate
    bonus_o[...] = bonus
    lw_o[...] = lw
    kka_o[:, 0:bw] = kk * a[:, 0:bw]
    kka_o[:, bw:] = kk * a[:, bw:]
    kd_o[:, 0:bw] = kd0
    kd_o[:, bw:] = kd1


def _rwkv_front(x, g, shift, scale, w_b, seq, consts, *, rows_per_mod, tm=256):
    m, d = x.shape
    pw = w_b.shape[1]
    (mu, w0, w2cat, a0, a2cat, g2p, k_k, k_a, r_k, e) = consts
    bw = k_k.shape[1]
    wl2, al2 = w2cat.shape[0], a2cat.shape[0]
    tm = min(tm, seq)
    tpb = rows_per_mod // tm
    nblk = tm // ROW_BLOCK
    last_blk = m // ROW_BLOCK - 1
    full = lambda arr: pl.BlockSpec(arr.shape, lambda i: (0,) * arr.ndim)
    mod_spec = pl.BlockSpec((None, 1, d), lambda i: (i // tpb, 0, 0))
    body = functools.partial(_prep_body, bw=bw, wl2=wl2, al2=al2, seq=seq, tm=tm)
    one = jax.ShapeDtypeStruct((m, bw), F32)
    two = jax.ShapeDtypeStruct((m, 2 * bw), F32)
    o1 = pl.BlockSpec((tm, bw), lambda i: (i, 0))
    o2 = pl.BlockSpec((tm, 2 * bw), lambda i: (i, 0))
    return pl.pallas_call(
        body,
        grid=(m // tm,),
        in_specs=[
            pl.BlockSpec((tm, d), lambda i: (i, 0)),
            pl.BlockSpec((ROW_BLOCK, d), lambda i: (jnp.maximum(i * nblk - 1, 0), 0)),
            pl.BlockSpec((ROW_BLOCK, d), lambda i: (jnp.minimum((i + 1) * nblk, last_blk), 0)),
            pl.BlockSpec((1, d), lambda i: (0, 0)),
            mod_spec, mod_spec,
            pl.BlockSpec((d, pw), lambda i: (0, 0), pipeline_mode=pl.Buffered(1)),
            full(mu), full(w0), full(w2cat), full(a0), full(a2cat), full(g2p), full(k_k), full(k_a),
            full(r_k), full(e),
        ],
        out_specs=[o1, o1, o1, o1, o1, o2, o2, o2],
        out_shape=[one, one, one, one, one, two, two, two],
        scratch_shapes=[pltpu.VMEM((tm + 2 * ROW_BLOCK, d), BF16), pltpu.VMEM((tm + 2 * ROW_BLOCK, pw), F32)],
        compiler_params=_cparams(("arbitrary",)),
    )(x, x, x, g.reshape(1, d), shift, scale, w_b, mu, w0, w2cat, a0, a2cat, g2p, k_k, k_a, r_k, e)


SCAN_CHUNK = 64
INV_BLOCK = 16


def _dot_nt(a, b):
    return lax.dot_general(a, b, (((1,), (1,)), ((), ())), preferred_element_type=F32)


def _dot_tn(a, b):
    return lax.dot_general(a, b, (((0,), (0,)), ((), ())), preferred_element_type=F32)


def _pair_blockdiag(x, lane_lo):
    return jnp.concatenate([jnp.where(lane_lo, x, 0.0), jnp.where(lane_lo, 0.0, x)], axis=0).astype(BF16)


def _chunk_scan_body(r_ref, v_ref, kk_ref, lw_ref, kka_ref, kd_ref, s0_ref, y_ref, sout_ref,
                     s_ref, qw_ref, u0_ref, arb_ref, pe_ref, vb_ref, elt_ref, *, reverse, nsub, pairs, group):
    c = SCAN_CHUNK
    step = pl.program_id(1)
    bf = lambda t: t.astype(BF16)

    @pl.when(step == 0)
    def _():
        s_ref[...] = s0_ref[...]

    ri = lax.broadcasted_iota(jnp.int32, (c, c), 0)
    ci = lax.broadcasted_iota(jnp.int32, (c, c), 1)
    tri = ((ri <= ci) if reverse else (ri >= ci)).astype(BF16)
    rp = lax.broadcasted_iota(jnp.int32, (c, LANE), 0)
    cp = lax.broadcasted_iota(jnp.int32, (c, LANE), 1) % c
    incl_p = (rp <= cp) if reverse else (rp >= cp)
    strict_p = (rp < cp) if reverse else (rp > cp)
    r2 = lax.broadcasted_iota(jnp.int32, (2 * c, LANE), 0)
    c2 = lax.broadcasted_iota(jnp.int32, (2 * c, LANE), 1) % c
    t2 = r2 % c
    mask_kr = ((t2 < c2) if reverse else (t2 > c2)) | ((r2 >= c) & (t2 == c2))
    blk_p = (rp // INV_BLOCK) == (cp // INV_BLOCK)
    eye_p = (rp == cp).astype(F32)
    lane_lo = lax.broadcasted_iota(jnp.int32, (c, LANE), 1) < HEAD_DIM
    bi = lax.broadcasted_iota(jnp.int32, (LANE, LANE), 0) // HEAD_DIM
    bj = lax.broadcasted_iota(jnp.int32, (LANE, LANE), 1) // HEAD_DIM
    same_head = bi == bj
    bd = lambda t: _pair_blockdiag(t, lane_lo)
    prange = range(pairs)
    lanes = [pl.ds(p * LANE, LANE) for p in prange]

    def phase_a(chunks):
        streams = [(j, p) for j in chunks for p in prange]
        sr = range(len(streams))
        rows = [pl.ds(j * c, c) for j, _ in streams]
        cols = [lanes[p] for _, p in streams]
        lw = [lw_ref[rows[s], cols[s]] for s in sr]
        lc = []
        for s in sr:
            hi = bf(lw[s])
            lc.append(_dot(tri, hi) + _dot(tri, bf(lw[s] - hi.astype(F32))))
        yield
        qq, qk_bd, qr, pk_bd, pb_bd, v_bd = [], [], [], [], [], []
        for s, (j, p) in enumerate(streams):
            ltot = lc[s][0:1] if reverse else lc[s][c - 1:c]
            e_neg = jnp.exp(-lc[s])
            e_end = jnp.exp(ltot - lc[s])
            kd = kd_ref[rows[s], cols[s]]
            kka = kka_ref[rows[s], cols[s]]
            v = v_ref[rows[s], cols[s]]
            qk = kk_ref[rows[s], cols[s]] * jnp.exp(lc[s] - lw[s])
            qr.append(r_ref[rows[s], cols[s]] * jnp.exp(lc[s]))
            qq.append(bf(jnp.concatenate([qk, qr[s]], axis=0)))
            qk_bd.append(bd(qk))
            pk_bd.append(bd(kd * e_neg))
            pb_bd.append(bd(kka * e_neg))
            v_bd.append(bd(v))
            pe_ref[j, p] = bf(jnp.concatenate([kd * e_end, -(kka * e_end)], axis=0))
            vb_ref[j, p] = bf(v)
            elt_ref[j, p] = jnp.broadcast_to(jnp.exp(ltot), (SUBLANE, LANE))
        g = [_dot_nt(qq[s], jnp.concatenate([pk_bd[s], pb_bd[s]], axis=0)) for s in sr]
        yield
        a_k = [bf(jnp.where(mask_kr, g[s][:, :LANE], 0.0)) for s in sr]
        nmat = [jnp.where(strict_p, g[s][:c, LANE:], 0.0) for s in sr]
        for s, (j, p) in enumerate(streams):
            arb_ref[j, p] = bf(jnp.where(incl_p, g[s][c:, LANE:], 0.0))
        avy = [_dot(a_k[s], v_bd[s]) for s in sr]
        av = [avy[s][:c] for s in sr]
        for s in sr:
            y_ref[rows[s], cols[s]] = avy[s][c:]
        nd = [jnp.where(blk_p, nmat[s], 0.0) for s in sr]
        lo_bd = [bd(nmat[s] - nd[s]) for s in sr]
        dinv = [eye_p - nd[s] for s in sr]
        mpow = [_dot(bf(-nd[s]), bd(-nd[s])) for s in sr]
        yield
        for _ in range(int(math.log2(INV_BLOCK)) - 2):
            both = [_dot(bf(jnp.concatenate([mpow[s], dinv[s]], axis=0)), bd(mpow[s])) for s in sr]
            mpow = [both[s][:c] for s in sr]
            dinv = [dinv[s] + both[s][c:] for s in sr]
            yield
        dinv = [dinv[s] + _dot(bf(dinv[s]), bd(mpow[s])) for s in sr]
        yield
        dinv_b = [bf(dinv[s]) for s in sr]
        x1 = [_dot(dinv_b[s], lo_bd[s]) for s in sr]
        yield
        x1b = [bf(x1[s]) for s in sr]
        acc = [eye_p - x1[s] for s in sr]
        xp = x1
        for k in range(2, SCAN_CHUNK // INV_BLOCK):
            xp = [_dot(x1b[s], bd(xp[s])) for s in sr]
            yield
            acc = [acc[s] + xp[s] if k % 2 == 0 else acc[s] - xp[s] for s in sr]
        tinv = [bf(_dot(bf(acc[s]), bd(dinv[s]))) for s in sr]
        yield
        wt = [_dot(tinv[s], qk_bd[s]) for s in sr]
        for s, (j, p) in enumerate(streams):
            u0_ref[j, p] = _dot(tinv[s], bd(av[s]))
            qw_ref[j, p] = bf(jnp.concatenate([wt[s], qr[s]], axis=0))

    def phase_b(chunks):
        for jj in chunks:
            rows = pl.ds(jj * c, c)
            s = [s_ref[p] for p in prange]
            xs = [_dot_nt(qw_ref[jj, p], bf(s[p])) for p in prange]
            yield
            u = [u0_ref[jj, p] + xs[p][:c] for p in prange]
            au = [_dot(arb_ref[jj, p], bd(u[p])) for p in prange]
            upd = [_dot_tn(jnp.concatenate([vb_ref[jj, p], bf(u[p])], axis=0), pe_ref[jj, p]) for p in prange]
            yield
            for p in prange:
                y_ref[rows, lanes[p]] = y_ref[rows, lanes[p]] + xs[p][c:] - au[p]
                s_ref[p] = s[p] * elt_ref[jj, p][0:1] + jnp.where(same_head, upd[p], 0.0)
            yield

    order = list(range(nsub))[::-1] if reverse else list(range(nsub))
    groups = [order[i:i + group] for i in range(0, nsub, group)]
    pending = None
    for chunks in groups + [None]:
        live = [gen for gen in (phase_a(chunks) if chunks is not None else None, pending) if gen is not None]
        while live:
            for gen in list(live):
                try:
                    next(gen)
                except StopIteration:
                    live.remove(gen)
        pending = phase_b(chunks) if chunks is not None else None

    @pl.when(step == pl.num_programs(1) - 1)
    def _():
        sout_ref[...] = s_ref[...]


def _chunk_scan(r, v, kk, lw, kka, kd, s0, *, bn, reverse, tt=512):
    m, bw = r.shape
    seq = m // bn
    tt = min(tt, seq)
    steps = seq // tt
    pairs = bw // LANE
    d = 1 if reverse else 0

    def row_blk(b, s):
        return b * steps + (steps - 1 - s if reverse else s)

    tok = pl.BlockSpec((tt, bw), lambda b, s: (row_blk(b, s), 0))
    tok_dir = pl.BlockSpec((tt, bw), lambda b, s: (row_blk(b, s), d))
    st = pl.BlockSpec((None, pairs, LANE, LANE), lambda b, s: (b, 0, 0, 0))
    c = SCAN_CHUNK
    nsub = tt // c
    group = 2 if nsub % 2 == 0 else 1
    body = functools.partial(_chunk_scan_body, reverse=reverse, nsub=nsub, pairs=pairs, group=group)
    return pl.pallas_call(
        body,
        grid=(bn, steps),
        in_specs=[tok, tok, tok, tok_dir, tok_dir, tok_dir, st],
        out_specs=[tok, st],
        out_shape=[jax.ShapeDtypeStruct((m, bw), F32), jax.ShapeDtypeStruct(s0.shape, F32)],
        scratch_shapes=[
            pltpu.VMEM((pairs, LANE, LANE), F32),
            pltpu.VMEM((nsub, pairs, 2 * c, LANE), BF16),
            pltpu.VMEM((nsub, pairs, c, LANE), F32),
            pltpu.VMEM((nsub, pairs, c, LANE), BF16),
            pltpu.VMEM((nsub, pairs, 2 * c, LANE), BF16),
            pltpu.VMEM((nsub, pairs, c, LANE), BF16),
            pltpu.VMEM((nsub, pairs, SUBLANE, LANE), F32),
        ],
        compiler_params=_cparams(("arbitrary", "arbitrary")),
    )(r, v, kk, lw, kka, kd, s0)


def _mixout_body(x_ref, gt_ref, ya_ref, yf_ref, yb_ref, bonus_ref, g_ref, lng_ref, lnb_ref, e_ref, wa_ref,
                 wb_ref, o_ref):
    e = e_ref[...]
    inv_n = 1.0 / HEAD_DIM
    y = yf_ref[...] + yb_ref[...]
    yc = y - _head_sums(y, e) * inv_n
    var = _head_sums(yc * yc, e) * inv_n
    yn = yc * lax.rsqrt(var + GN_EPS) * lng_ref[...] + lnb_ref[...]
    yb = ((yn + bonus_ref[...]) * g_ref[...]).astype(BF16)
    ox = _dot(ya_ref[...], wa_ref[...]) + _dot(yb, wb_ref[...])
    o_ref[...] = x_ref[...] + gt_ref[...] * ox


def _mix_out(x, gate, ya, y_fwd, y_bwd, bonus, g, ln_g, ln_b, e, w_out_a, w_out_b, *, rows_per_mod, tm=256):
    m, d = x.shape
    aw, bw = ya.shape[1], y_fwd.shape[1]
    tpb = rows_per_mod // tm
    full = lambda arr: pl.BlockSpec(arr.shape, lambda i: (0,) * arr.ndim)
    tok = lambda width: pl.BlockSpec((tm, width), lambda i: (i, 0))
    return pl.pallas_call(
        _mixout_body,
        grid=(m // tm,),
        in_specs=[
            tok(d), pl.BlockSpec((None, 1, d), lambda i: (i // tpb, 0, 0)),
            tok(aw), tok(bw), tok(bw), tok(bw), tok(bw),
            full(ln_g), full(ln_b), full(e), full(w_out_a), full(w_out_b),
        ],
        out_specs=tok(d),
        out_shape=jax.ShapeDtypeStruct((m, d), F32),
        compiler_params=_cparams(("arbitrary",)),
    )(x, gate, ya, y_fwd, y_bwd, bonus, g, ln_g, ln_b, e, w_out_a, w_out_b)


def kernel(x, c, ctx, c_ctx, ada_w, ada_b, norm_g, ffn_w1, ffn_w3, ffn_w2, ab_w_in, ab_w_out, gm_v_gain,
           gm_ws, gm_b, rw_mu, rw_w0, rw_w2, rw_a0, rw_a2, rw_g2, rw_k_k, rw_k_a, rw_r_k, rw_ln_g, rw_ln_b,
           sc_w_in, sc_conv, sc_w_out, final_g):
    bn, seq, d = x.shape
    ctx_len = ctx.shape[1]
    depth = ada_w.shape[0]
    n_mod = ada_w.shape[2] // d
    m_x, m_c = bn * seq, bn * ctx_len
    xs = x.reshape(m_x, d)
    cs = ctx.reshape(m_c, d)

    rows = _round_up(bn + 1, SUBLANE)
    c_rows = jnp.zeros((rows, d), F32).at[:bn].set(c).at[bn].set(c_ctx)
    mods = _modulation(c_rows, ada_w, ada_b).reshape(depth, rows, n_mod, d)

    sc_in, sc_out = sc_w_in.astype(BF16), sc_w_out.astype(BF16)

    ffn_order = [(i, half) for i in range(depth) for half in (0, 1)]
    ffn_f32 = (ffn_w1, ffn_w3, ffn_w2)
    ffn_bf16 = {ffn_order[0]: tuple(_cast_set(w, ffn_order[0]) for w in ffn_f32)}
    ffn_tm, ffn_tf = 1024, 512
    ffn_tiles = m_x // min(ffn_tm, seq)
    can_piggyback = all(_cast_plan(w.shape[-2:], d, ffn_tiles, ffn_tf) is not None for w in ffn_f32)

    def ffn(t, key, sub, mod, rows_per_mod, final=None, prepare=None):
        i = key[0]
        w1, w3, w2 = ffn_bf16[key]
        cast = ()
        if prepare is not None:
            if can_piggyback:
                cast = tuple((w, prepare) for w in ffn_f32)
            else:
                ffn_bf16[prepare] = tuple(_cast_set(w, prepare) for w in ffn_f32)
        out, copies = _glu_block(t, norm_g[i, sub], mod[3 * sub], mod[3 * sub + 1], mod[3 * sub + 2],
                                 [(w1, (), 0), (w3, (), 0)], (w2, ()), kind="swiglu", coef=0.5, final_g=final,
                                 rows_per_mod=rows_per_mod, tm=ffn_tm, tf=ffn_tf if final is None else 256,
                                 cast=cast)
        if cast:
            ffn_bf16[prepare] = tuple(copies)
        return out

    def next_key(key):
        k = ffn_order.index(key) + 1
        return ffn_order[k] if k < len(ffn_order) else None

    for i in range(depth):
        even = i % 2 == 0
        ctx_after = any(j % 2 == 0 for j in range(i + 1, depth))
        use_ctx = cs is not None and (even or ctx_after)
        if not use_ctx:
            cs = None
        mx = [mods[i, :bn, j].reshape(bn, 1, d) for j in range(n_mod)]
        mc = [mods[i, bn:bn + 1, j].reshape(1, 1, d) for j in range(n_mod)]
        last_layer = i == depth - 1

        xs = ffn(xs, (i, 0), 0, mx, seq, prepare=next_key((i, 0)))
        if cs is not None:
            cs = ffn(cs, (i, 0), 0, mc, m_c)

        if even:
            e_idx = i // 2
            if ctx_after:
                raise NotImplementedError("context output of an even layer is only needed for depth > 2")
            xs = _ab_mixer(xs, cs, mx, mc, norm_g[i, 1], bn, seq, ctx_len, ab_w_in[e_idx], ab_w_out[e_idx],
                           gm_v_gain[e_idx], gm_ws[e_idx], gm_b[e_idx], rw_mu[e_idx], rw_w0[e_idx],
                           rw_w2[e_idx], rw_a0[e_idx], rw_a2[e_idx], rw_g2[e_idx], rw_k_k[e_idx],
                           rw_k_a[e_idx], rw_r_k[e_idx], rw_ln_g[e_idx], rw_ln_b[e_idx])
        else:
            o_idx = i // 2
            if ctx_after:
                raise NotImplementedError("context output of an odd layer is only needed for depth > 2")
            ups = [(sc_in, (o_idx,), k * d) for k in range(3)]
            xs, _ = _glu_block(xs, norm_g[i, 1], mx[3], mx[4], mx[5], ups, (sc_out, (o_idx,)), kind="conv",
                               coef=1.0, conv_w=(sc_conv, (o_idx,)), rows_per_mod=seq, tm=512)
        cs = None

        xs = ffn(xs, (i, 1), 2, mx, seq, final=final_g if last_layer else None, prepare=next_key((i, 1)))
    return xs.reshape(bn, seq, d)


def _ab_mixer(xs, cs, mx, mc, g, bn, seq, ctx_len, w_in, w_out, v_gain, ws, b_s, mu, w0, w2, a0, a2, g2,
              k_k, k_a, r_k, ln_g, ln_b):
    m_x, d = xs.shape
    m_c = cs.shape[0]
    bw = k_k.shape[0]
    aw = v_gain.shape[0]
    a_cols = 2 * aw
    heads = bw // HEAD_DIM
    wl, al, gl = w2.shape[1], a2.shape[1], g2.shape[0]
    b_cols = w_in.shape[1] - a_cols
    pw = _round_up(b_cols, 512)
    gp = pw - (3 * bw + 2 * wl + 2 * al)

    w_in_bf = w_in.astype(BF16)
    w_in_a = w_in_bf[:, :a_cols]
    w_in_b = jnp.pad(w_in_bf[:, a_cols:], ((0, 0), (0, pw - b_cols)))

    mu_p = jnp.pad(mu, ((0, 0), (0, pw - b_cols)))
    zeros_w = jnp.zeros((wl, bw), F32)
    zeros_a = jnp.zeros((al, bw), F32)
    w2cat = jnp.concatenate([jnp.concatenate([w2[0], zeros_w], 1), jnp.concatenate([zeros_w, w2[1]], 1)], 0)
    a2cat = jnp.concatenate([jnp.concatenate([a2[0], zeros_a], 1), jnp.concatenate([zeros_a, a2[1]], 1)], 0)
    g2p = jnp.pad(g2, ((0, gp - gl), (0, 0)))
    head_id = jnp.arange(min(bw, MXU_WIDTH)) // HEAD_DIM
    e = (head_id[:, None] == head_id[None, :]).astype(BF16)
    consts = (mu_p, w0.reshape(1, 2 * bw), w2cat.astype(BF16), a0.reshape(1, 2 * bw), a2cat.astype(BF16),
              g2p.astype(BF16), k_k.reshape(1, bw), k_a.reshape(1, bw), r_k.reshape(1, bw), e)

    b_full = jnp.repeat(b_s.T, aw // ws.shape[0], axis=1)
    ya = _gmlp_branch(xs, g, mx[3], mx[4], w_in_a, v_gain, ws.astype(BF16), b_full, rows_per_mod=seq)

    fx = _rwkv_front(xs, g, mx[3], mx[4], w_in_b, seq, consts, rows_per_mod=seq)
    fc = _rwkv_front(cs, g, mc[3], mc[4], w_in_b, ctx_len, consts, rows_per_mod=m_c)
    r_x, v_x, kk_x, g_x, bonus_x, lw_x, kka_x, kd_x = fx
    r_c, v_c, kk_c, _, _, lw_c, kka_c, kd_c = fc
    s_zero = jnp.zeros((bn, bw // LANE, LANE, LANE), F32)
    ys = []
    for reverse in (False, True):
        _, s_ctx = _chunk_scan(r_c, v_c, kk_c, lw_c, kka_c, kd_c, s_zero, bn=bn, reverse=reverse)
        y_dir, _ = _chunk_scan(r_x, v_x, kk_x, lw_x, kka_x, kd_x, s_ctx, bn=bn, reverse=reverse)
        ys.append(y_dir)

    w_out_bf = w_out.astype(BF16)
    return _mix_out(xs, mx[5], ya, ys[0], ys[1], bonus_x, g_x, ln_g.reshape(1, bw), ln_b.reshape(1, bw), e,
                    w_out_bf[:aw], w_out_bf[aw:], rows_per_mod=seq)
```

```python
import functools
import math

import jax
import jax.numpy as jnp
from jax import lax
from jax.experimental import pallas as pl
from jax.experimental.pallas import tpu as pltpu

F32 = jnp.float32
BF16 = jnp.bfloat16

GRID_W = 64
CHUNK = 128
HEAD_DIM = 64
RMS_EPS = 1e-6
LN_EPS = 1e-5
GN_EPS = 64e-5
KK_EPS = 1e-12
W_DECAY_SCALE = math.exp(-0.5)

LANE = 128
SUBLANE = 8
MXU_WIDTH = 256
VMEM_LIMIT = 58 * 1024 * 1024


def _cparams(sem, limit=None):
    return pltpu.CompilerParams(dimension_semantics=sem, vmem_limit_bytes=VMEM_LIMIT if limit is None else limit)


def _round_up(n, m):
    return (n + m - 1) // m * m


def _dot(a, b):
    return jnp.dot(a, b, preferred_element_type=F32)


def _dot2(a, b):
    hi = a.astype(BF16)
    lo = (a - hi.astype(F32)).astype(BF16)
    return _dot(hi, b) + _dot(lo, b)


def _head_sums(x, e):
    blk = e.shape[0]
    parts = [_dot2(x[:, j:j + blk], e) for j in range(0, x.shape[1], blk)]
    return parts[0] if len(parts) == 1 else jnp.concatenate(parts, axis=1)


ROW_BLOCK = 16
ROW_UNROLL = 8
EPILOGUE_ROWS = 128


def _prenorm_rows(x_ref, g, shift, scale, h_ref, *, src_row=0, dst_row=0, n_rows=None, zero_ref=None):
    n_rows = x_ref.shape[0] if n_rows is None else n_rows
    gs = g * (1.0 + scale)

    def block(i, carry):
        off = pl.multiple_of(i * ROW_BLOCK, ROW_BLOCK)
        x = x_ref[pl.ds(src_row + off, ROW_BLOCK), :]
        ms = jnp.mean(x * x, axis=-1, keepdims=True)
        h_ref[pl.ds(dst_row + off, ROW_BLOCK), :] = (x * lax.rsqrt(ms + RMS_EPS) * gs + shift).astype(h_ref.dtype)
        if zero_ref is not None:
            zero_ref[pl.ds(dst_row + off, ROW_BLOCK), :] = jnp.zeros((ROW_BLOCK, zero_ref.shape[1]), zero_ref.dtype)
        return carry

    n_blocks = n_rows // ROW_BLOCK
    lax.fori_loop(0, n_blocks, block, 0, unroll=math.gcd(n_blocks, ROW_UNROLL))


def _mod_body(c_ref, w_ref, b_ref, o_ref):
    s = jax.nn.silu(c_ref[...]).astype(BF16)
    o_ref[...] = _dot(s, w_ref[...].astype(BF16)) + b_ref[...]


def _modulation(c_rows, ada_w, ada_b):
    depth, d, n = ada_w.shape
    tn = 1024 if n % 1024 == 0 else 512
    rows = c_rows.shape[0]
    return pl.pallas_call(
        _mod_body,
        grid=(depth, n // tn),
        in_specs=[
            pl.BlockSpec((rows, d), lambda l, j: (0, 0)),
            pl.BlockSpec((None, d, tn), lambda l, j: (l, 0, j)),
            pl.BlockSpec((None, 1, tn), lambda l, j: (l, 0, j)),
        ],
        out_specs=pl.BlockSpec((None, rows, tn), lambda l, j: (l, 0, j)),
        out_shape=jax.ShapeDtypeStruct((depth, rows, n), F32),
        compiler_params=_cparams(("arbitrary", "arbitrary")),
    )(c_rows, ada_w, ada_b.reshape(depth, 1, n))


def _glu_body(*refs, n_up, kind, coef, final, nf, n_cast):
    x_ref, g_ref, sh_ref, sc_ref, gt_ref = refs[:5]
    up_refs = refs[5:5 + n_up]
    down_ref = refs[5 + n_up]
    idx = 6 + n_up
    cw_ref = fg_ref = None
    if kind == "conv":
        cw_ref = refs[idx]
        idx += 1
    if final:
        fg_ref = refs[idx]
        idx += 1
    cast_in = refs[idx:idx + n_cast]
    idx += n_cast
    o_ref = refs[idx]
    cast_out = refs[idx + 1:idx + 1 + n_cast]
    h_ref = refs[idx + 1 + n_cast]
    f = pl.program_id(1)

    for ci, co in zip(cast_in, cast_out):
        co[...] = ci[...].astype(co.dtype)

    @pl.when(f == 0)
    def _():
        _prenorm_rows(x_ref, g_ref[...], sh_ref[...], sc_ref[...], h_ref, zero_ref=o_ref)

    h = h_ref[...]
    ups = [_dot(h, u[...]) for u in up_refs]
    if kind == "swiglu":
        mid = jax.nn.silu(ups[0]) * ups[1]
    else:
        z = ups[1] * ups[2]
        rows = z.shape[0]
        col = lax.broadcasted_iota(jnp.int32, z.shape, 0) % GRID_W
        zp = jnp.where(col == 0, 0.0, pltpu.roll(z, 1, axis=0))
        zn = jnp.where(col == GRID_W - 1, 0.0, pltpu.roll(z, rows - 1, axis=0))
        cw = cw_ref[...]
        mid = ups[0] * (cw[0:1] * zp + cw[1:2] * z + cw[2:3] * zn)
    o_ref[...] += _dot(mid.astype(BF16), down_ref[...])

    @pl.when(f == nf - 1)
    def _():
        cg = coef * gt_ref[...]
        tm = o_ref.shape[0]
        step = min(tm, EPILOGUE_ROWS)
        for r0 in range(0, tm, step):
            rows = slice(r0, r0 + step)
            res = x_ref[rows, :] + cg * o_ref[rows, :]
            if final:
                ms = jnp.mean(res * res, axis=-1, keepdims=True)
                res = res * lax.rsqrt(ms + RMS_EPS) * fg_ref[...]
            o_ref[rows, :] = res


def _cast_plan(mat, d, n_tiles, tf):
    if d % n_tiles or (d // n_tiles) % LANE:
        return None
    if mat[0] == d:
        return (d // n_tiles, tf), (lambda i, f: (i, f))
    return (tf, d // n_tiles), (lambda i, f: (f, i))


def _glu_block(x, g, shift, scale, gate, ups, down, *, kind, coef, conv_w=None, final_g=None,
               rows_per_mod, tm=1024, tf=512, cast=(), vmem_limit=None):
    m, d = x.shape
    down_w, down_lead = down
    fdim = down_w.shape[-2]
    tm = min(tm, m, rows_per_mod)
    nf = fdim // tf
    tpb = rows_per_mod // tm
    mod_spec = pl.BlockSpec((None, 1, d), lambda i, f: (i // tpb, 0, 0))
    in_specs = [
        pl.BlockSpec((tm, d), lambda i, f: (i, 0)),
        pl.BlockSpec((1, d), lambda i, f: (0, 0)),
        mod_spec, mod_spec, mod_spec,
    ]
    args = [x, g.reshape(1, d), shift, scale, gate]
    for w, lead, col0 in ups:
        blk0 = col0 // tf
        in_specs.append(pl.BlockSpec((None,) * len(lead) + (d, tf),
                                     lambda i, f, lead=lead, blk0=blk0: lead + (0, blk0 + f)))
        args.append(w)
    in_specs.append(pl.BlockSpec((None,) * len(down_lead) + (tf, d), lambda i, f: down_lead + (f, 0)))
    args.append(down_w)
    if kind == "conv":
        cw, cw_lead = conv_w
        in_specs.append(pl.BlockSpec((None,) * len(cw_lead) + (3, tf), lambda i, f: cw_lead + (0, f)))
        args.append(cw)
    if final_g is not None:
        in_specs.append(pl.BlockSpec((1, d), lambda i, f: (0, 0)))
        args.append(final_g.reshape(1, d))
    steps = (m // tm) * nf
    out_specs = [pl.BlockSpec((tm, d), lambda i, f: (i, 0))]
    out_shape = [jax.ShapeDtypeStruct((m, d), F32)]
    for w, lead in cast:
        blk, order = _cast_plan(w.shape[-2:], d, m // tm, tf)
        in_specs.append(pl.BlockSpec((None,) * len(lead) + blk,
                                     lambda i, f, lead=lead, order=order: lead + order(i, f)))
        args.append(w)
        out_specs.append(pl.BlockSpec(blk, order))
        out_shape.append(jax.ShapeDtypeStruct(w.shape[-2:], BF16))
    body = functools.partial(_glu_body, n_up=len(ups), kind=kind, coef=coef,
                             final=final_g is not None, nf=nf, n_cast=len(cast))
    res = pl.pallas_call(
        body,
        grid=(m // tm, nf),
        in_specs=in_specs,
        out_specs=out_specs,
        out_shape=out_shape,
        scratch_shapes=[pltpu.VMEM((tm, d), BF16)],
        compiler_params=_cparams(("arbitrary", "arbitrary"), vmem_limit),
    )(*args)
    return res[0], list(res[1:])


def _cast_body(i_ref, o_ref):
    o_ref[...] = i_ref[...].astype(o_ref.dtype)


def _cast_set(w, lead, rows=512):
    n_rows, n_cols = w.shape[-2:]
    rows = math.gcd(rows, n_rows)
    return pl.pallas_call(
        _cast_body,
        grid=(n_rows // rows,),
        in_specs=[pl.BlockSpec((None,) * len(lead) + (rows, n_cols), lambda i: lead + (i, 0))],
        out_specs=pl.BlockSpec((rows, n_cols), lambda i: (i, 0)),
        out_shape=jax.ShapeDtypeStruct((n_rows, n_cols), BF16),
        compiler_params=_cparams(("arbitrary",)),
    )(w)


def _gmlp_body(x_ref, g_ref, sh_ref, sc_ref, w_ref, gain_ref, ws_ref, b_ref, o_ref, h_ref, p_ref, *,
               aw, groups, n_chunks):
    _prenorm_rows(x_ref, g_ref[...], sh_ref[...], sc_ref[...], h_ref)
    p_ref[...] = _dot(h_ref[...], w_ref[...])
    gd = aw // groups
    for c in range(n_chunks):
        rows = pl.ds(c * CHUNK, CHUNK)
        u = jax.nn.gelu(p_ref[rows, 0:aw])
        v = jax.nn.gelu(p_ref[rows, aw:2 * aw])
        vm = jnp.mean(v, axis=-1, keepdims=True)
        vc = v - vm
        vv = jnp.mean(vc * vc, axis=-1, keepdims=True)
        vn = (vc * lax.rsqrt(vv + LN_EPS) * gain_ref[...]).astype(BF16)
        for gi in range(groups):
            cols = slice(gi * gd, (gi + 1) * gd)
            vs = _dot(ws_ref[gi], vn[:, cols]) + b_ref[:, cols]
            o_ref[rows, cols] = (u[:, cols] * vs).astype(o_ref.dtype)


def _gmlp_branch(x, g, shift, scale, w_a, gain, ws, b_full, *, rows_per_mod, tm=512):
    m, d = x.shape
    two_aw = w_a.shape[1]
    aw = two_aw // 2
    groups = ws.shape[0]
    tm = min(tm, m, rows_per_mod)
    tpb = rows_per_mod // tm
    mod_spec = pl.BlockSpec((None, 1, d), lambda i: (i // tpb, 0, 0))
    body = functools.partial(_gmlp_body, aw=aw, groups=groups, n_chunks=tm // CHUNK)
    return pl.pallas_call(
        body,
        grid=(m // tm,),
        in_specs=[
            pl.BlockSpec((tm, d), lambda i: (i, 0)),
            pl.BlockSpec((1, d), lambda i: (0, 0)),
            mod_spec, mod_spec,
            pl.BlockSpec((d, two_aw), lambda i: (0, 0), pipeline_mode=pl.Buffered(1)),
            pl.BlockSpec((1, aw), lambda i: (0, 0)),
            pl.BlockSpec((groups, CHUNK, CHUNK), lambda i: (0, 0, 0)),
            pl.BlockSpec((CHUNK, aw), lambda i: (0, 0)),
        ],
        out_specs=pl.BlockSpec((tm, aw), lambda i: (i, 0)),
        out_shape=jax.ShapeDtypeStruct((m, aw), BF16),
        scratch_shapes=[pltpu.VMEM((tm, d), BF16), pltpu.VMEM((tm, two_aw), F32)],
        compiler_params=_cparams(("arbitrary",)),
    )(x, g.reshape(1, d), shift, scale, w_a, gain.reshape(1, aw), ws, b_full)


def _prep_body(x_ref, xp_ref, xn_ref, g_ref, sh_ref, sc_ref, w_ref, mu_ref, w0_ref, w2_ref, a0_ref, a2_ref,
               g2_ref, kk_ref, ka_ref, rk_ref, e_ref, r_o, v_o, kk_o, g_o, bonus_o, lw_o, kka_o, kd_o,
               h_ref, p_ref, *, bw, wl2, al2, seq, tm):
    i = pl.program_id(0)
    g, sh, sc = g_ref[...], sh_ref[...], sc_ref[...]
    _prenorm_rows(x_ref, g, sh, sc, h_ref)
    _prenorm_rows(xp_ref, g, sh, sc, h_ref, dst_row=tm)
    _prenorm_rows(xn_ref, g, sh, sc, h_ref, dst_row=tm + ROW_BLOCK)
    p_ref[...] = _dot(h_ref[...], w_ref[...])

    p = p_ref[0:tm, :]
    row = lax.broadcasted_iota(jnp.int32, p.shape, 0)
    first = (i * tm) % seq == 0
    last = ((i + 1) * tm) % seq == 0
    prev_row = jnp.where(first, 0.0, p_ref[tm + ROW_BLOCK - 1:tm + ROW_BLOCK, :])
    next_row = jnp.where(last, 0.0, p_ref[tm + ROW_BLOCK:tm + ROW_BLOCK + 1, :])
    prev = jnp.where(row == 0, prev_row, pltpu.roll(p, 1, axis=0))
    nxt = jnp.where(row == tm - 1, next_row, pltpu.roll(p, tm - 1, axis=0))
    mu = mu_ref[...]
    p = p + mu[0:1] * (prev - p) + mu[1:2] * (nxt - p)

    r = p[:, 0:bw]
    k = p[:, bw:2 * bw]
    v = p[:, 2 * bw:3 * bw]
    o = 3 * bw
    wd = jnp.tanh(p[:, o:o + wl2]).astype(BF16)
    ad = p[:, o + wl2:o + wl2 + al2].astype(BF16)
    gdn = jax.nn.sigmoid(p[:, o + wl2 + al2:]).astype(BF16)

    lw = -W_DECAY_SCALE * jax.nn.sigmoid(w0_ref[...] + _dot(wd, w2_ref[...]))
    a = jax.nn.sigmoid(a0_ref[...] + _dot(ad, a2_ref[...]))
    gate = _dot(gdn, g2_ref[...])

    e = e_ref[...]
    kq = k * kk_ref[...]
    kk = kq * lax.rsqrt(_head_sums(kq * kq, e) + KK_EPS)
    ka = ka_ref[...]
    kd0 = k * (1.0 + (a[:, 0:bw] - 1.0) * ka)
    kd1 = k * (1.0 + (a[:, bw:] - 1.0) * ka)
    kb = 0.5 * (kd0 + kd1)
    bonus = _head_sums(r * kb * rk_ref[...], e) * v

    r_o[...] = r
    v_o[...] = v
    kk_o[...] = kk
    g_o[...] = gate
    bonus_o[...] = bonus
    lw_o[...] = lw
    kka_o[:, 0:bw] = kk * a[:, 0:bw]
    kka_o[:, bw:] = kk * a[:, bw:]
    kd_o[:, 0:bw] = kd0
    kd_o[:, bw:] = kd1


def _rwkv_front(x, g, shift, scale, w_b, seq, consts, *, rows_per_mod, tm=256):
    m, d = x.shape
    pw = w_b.shape[1]
    (mu, w0, w2cat, a0, a2cat, g2p, k_k, k_a, r_k, e) = consts
    bw = k_k.shape[1]
    wl2, al2 = w2cat.shape[0], a2cat.shape[0]
    tm = min(tm, seq)
    tpb = rows_per_mod // tm
    nblk = tm // ROW_BLOCK
    last_blk = m // ROW_BLOCK - 1
    full = lambda arr: pl.BlockSpec(arr.shape, lambda i: (0,) * arr.ndim)
    mod_spec = pl.BlockSpec((None, 1, d), lambda i: (i // tpb, 0, 0))
    body = functools.partial(_prep_body, bw=bw, wl2=wl2, al2=al2, seq=seq, tm=tm)
    one = jax.ShapeDtypeStruct((m, bw), F32)
    two = jax.ShapeDtypeStruct((m, 2 * bw), F32)
    o1 = pl.BlockSpec((tm, bw), lambda i: (i, 0))
    o2 = pl.BlockSpec((tm, 2 * bw), lambda i: (i, 0))
    return pl.pallas_call(
        body,
        grid=(m // tm,),
        in_specs=[
            pl.BlockSpec((tm, d), lambda i: (i, 0)),
            pl.BlockSpec((ROW_BLOCK, d), lambda i: (jnp.maximum(i * nblk - 1, 0), 0)),
            pl.BlockSpec((ROW_BLOCK, d), lambda i: (jnp.minimum((i + 1) * nblk, last_blk), 0)),
            pl.BlockSpec((1, d), lambda i: (0, 0)),
            mod_spec, mod_spec,
            pl.BlockSpec((d, pw), lambda i: (0, 0), pipeline_mode=pl.Buffered(1)),
            full(mu), full(w0), full(w2cat), full(a0), full(a2cat), full(g2p), full(k_k), full(k_a),
            full(r_k), full(e),
        ],
        out_specs=[o1, o1, o1, o1, o1, o2, o2, o2],
        out_shape=[one, one, one, one, one, two, two, two],
        scratch_shapes=[pltpu.VMEM((tm + 2 * ROW_BLOCK, d), BF16), pltpu.VMEM((tm + 2 * ROW_BLOCK, pw), F32)],
        compiler_params=_cparams(("arbitrary",)),
    )(x, x, x, g.reshape(1, d), shift, scale, w_b, mu, w0, w2cat, a0, a2cat, g2p, k_k, k_a, r_k, e)


SCAN_CHUNK = 64
INV_BLOCK = 16


def _dot_nt(a, b):
    return lax.dot_general(a, b, (((1,), (1,)), ((), ())), preferred_element_type=F32)


def _dot_tn(a, b):
    return lax.dot_general(a, b, (((0,), (0,)), ((), ())), preferred_element_type=F32)


def _pair_blockdiag(x, lane_lo):
    return jnp.concatenate([jnp.where(lane_lo, x, 0.0), jnp.where(lane_lo, 0.0, x)], axis=0).astype(BF16)


def _chunk_scan_body(r_ref, v_ref, kk_ref, lw_ref, kka_ref, kd_ref, s0_ref, y_ref, sout_ref,
                     s_ref, qw_ref, u0_ref, arb_ref, pe_ref, vb_ref, elt_ref, *, reverse, nsub, pairs, group):
    c = SCAN_CHUNK
    step = pl.program_id(1)
    bf = lambda t: t.astype(BF16)

    @pl.when(step == 0)
    def _():
        s_ref[...] = s0_ref[...]

    ri = lax.broadcasted_iota(jnp.int32, (c, c), 0)
    ci = lax.broadcasted_iota(jnp.int32, (c, c), 1)
    tri = ((ri <= ci) if reverse else (ri >= ci)).astype(BF16)
    rp = lax.broadcasted_iota(jnp.int32, (c, LANE), 0)
    cp = lax.broadcasted_iota(jnp.int32, (c, LANE), 1) % c
    incl_p = (rp <= cp) if reverse else (rp >= cp)
    strict_p = (rp < cp) if reverse else (rp > cp)
    r2 = lax.broadcasted_iota(jnp.int32, (2 * c, LANE), 0)
    c2 = lax.broadcasted_iota(jnp.int32, (2 * c, LANE), 1) % c
    t2 = r2 % c
    mask_kr = ((t2 < c2) if reverse else (t2 > c2)) | ((r2 >= c) & (t2 == c2))
    blk_p = (rp // INV_BLOCK) == (cp // INV_BLOCK)
    eye_p = (rp == cp).astype(F32)
    lane_lo = lax.broadcasted_iota(jnp.int32, (c, LANE), 1) < HEAD_DIM
    bi = lax.broadcasted_iota(jnp.int32, (LANE, LANE), 0) // HEAD_DIM
    bj = lax.broadcasted_iota(jnp.int32, (LANE, LANE), 1) // HEAD_DIM
    same_head = bi == bj
    bd = lambda t: _pair_blockdiag(t, lane_lo)
    prange = range(pairs)
    lanes = [pl.ds(p * LANE, LANE) for p in prange]

    def phase_a(chunks):
        streams = [(j, p) for j in chunks for p in prange]
        sr = range(len(streams))
        rows = [pl.ds(j * c, c) for j, _ in streams]
        cols = [lanes[p] for _, p in streams]
        lw = [lw_ref[rows[s], cols[s]] for s in sr]
        lc = []
        for s in sr:
            hi = bf(lw[s])
            lc.append(_dot(tri, hi) + _dot(tri, bf(lw[s] - hi.astype(F32))))
        yield
        qq, qk_bd, qr, pk_bd, pb_bd, v_bd = [], [], [], [], [], []
        for s, (j, p) in enumerate(streams):
            ltot = lc[s][0:1] if reverse else lc[s][c - 1:c]
            e_neg = jnp.exp(-lc[s])
            e_end = jnp.exp(ltot - lc[s])
            kd = kd_ref[rows[s], cols[s]]
            kka = kka_ref[rows[s], cols[s]]
            v = v_ref[rows[s], cols[s]]
            qk = kk_ref[rows[s], cols[s]] * jnp.exp(lc[s] - lw[s])
            qr.append(r_ref[rows[s], cols[s]] * jnp.exp(lc[s]))
            qq.append(bf(jnp.concatenate([qk, qr[s]], axis=0)))
            qk_bd.append(bd(qk))
            pk_bd.append(bd(kd * e_neg))
            pb_bd.append(bd(kka * e_neg))
            v_bd.append(bd(v))
            pe_ref[j, p] = bf(jnp.concatenate([kd * e_end, -(kka * e_end)], axis=0))
            vb_ref[j, p] = bf(v)
            elt_ref[j, p] = jnp.broadcast_to(jnp.exp(ltot), (SUBLANE, LANE))
        g = [_dot_nt(qq[s], jnp.concatenate([pk_bd[s], pb_bd[s]], axis=0)) for s in sr]
        yield
        a_k = [bf(jnp.where(mask_kr, g[s][:, :LANE], 0.0)) for s in sr]
        nmat = [jnp.where(strict_p, g[s][:c, LANE:], 0.0) for s in sr]
        for s, (j, p) in enumerate(streams):
            arb_ref[j, p] = bf(jnp.where(incl_p, g[s][c:, LANE:], 0.0))
        avy = [_dot(a_k[s], v_bd[s]) for s in sr]
        av = [avy[s][:c] for s in sr]
        for s in sr:
            y_ref[rows[s], cols[s]] = avy[s][c:]
        nd = [jnp.where(blk_p, nmat[s], 0.0) for s in sr]
        lo_bd = [bd(nmat[s] - nd[s]) for s in sr]
        dinv = [eye_p - nd[s] for s in sr]
        mpow = [_dot(bf(-nd[s]), bd(-nd[s])) for s in sr]
        yield
        for _ in range(int(math.log2(INV_BLOCK)) - 2):
            both = [_dot(bf(jnp.concatenate([mpow[s], dinv[s]], axis=0)), bd(mpow[s])) for s in sr]
            mpow = [both[s][:c] for s in sr]
            dinv = [dinv[s] + both[s][c:] for s in sr]
            yield
        dinv = [dinv[s] + _dot(bf(dinv[s]), bd(mpow[s])) for s in sr]
        yield
        dinv_b = [bf(dinv[s]) for s in sr]
        x1 = [_dot(dinv_b[s], lo_bd[s]) for s in sr]
        yield
        x1b = [bf(x1[s]) for s in sr]
        acc = [eye_p - x1[s] for s in sr]
        xp = x1
        for k in range(2, SCAN_CHUNK // INV_BLOCK):
            xp = [_dot(x1b[s], bd(xp[s])) for s in sr]
            yield
            acc = [acc[s] + xp[s] if k % 2 == 0 else acc[s] - xp[s] for s in sr]
        tinv = [bf(_dot(bf(acc[s]), bd(dinv[s]))) for s in sr]
        yield
        wt = [_dot(tinv[s], qk_bd[s]) for s in sr]
        for s, (j, p) in enumerate(streams):
            u0_ref[j, p] = _dot(tinv[s], bd(av[s]))
            qw_ref[j, p] = bf(jnp.concatenate([wt[s], qr[s]], axis=0))

    def phase_b(chunks):
        for jj in chunks:
            rows = pl.ds(jj * c, c)
            s = [s_ref[p] for p in prange]
            xs = [_dot_nt(qw_ref[jj, p], bf(s[p])) for p in prange]
            yield
            u = [u0_ref[jj, p] + xs[p][:c] for p in prange]
            au = [_dot(arb_ref[jj, p], bd(u[p])) for p in prange]
            upd = [_dot_tn(jnp.concatenate([vb_ref[jj, p], bf(u[p])], axis=0), pe_ref[jj, p]) for p in prange]
            yield
            for p in prange:
                y_ref[rows, lanes[p]] = y_ref[rows, lanes[p]] + xs[p][c:] - au[p]
                s_ref[p] = s[p] * elt_ref[jj, p][0:1] + jnp.where(same_head, upd[p], 0.0)
            yield

    order = list(range(nsub))[::-1] if reverse else list(range(nsub))
    groups = [order[i:i + group] for i in range(0, nsub, group)]
    pending = None
    for chunks in groups + [None]:
        live = [gen for gen in (phase_a(chunks) if chunks is not None else None, pending) if gen is not None]
        while live:
            for gen in list(live):
                try:
                    next(gen)
                except StopIteration:
                    live.remove(gen)
        pending = phase_b(chunks) if chunks is not None else None

    @pl.when(step == pl.num_programs(1) - 1)
    def _():
        sout_ref[...] = s_ref[...]


def _chunk_scan(r, v, kk, lw, kka, kd, s0, *, bn, reverse, tt=512):
    m, bw = r.shape
    seq = m // bn
    tt = min(tt, seq)
    steps = seq // tt
    pairs = bw // LANE
    d = 1 if reverse else 0

    def row_blk(b, s):
        return b * steps + (steps - 1 - s if reverse else s)

    tok = pl.BlockSpec((tt, bw), lambda b, s: (row_blk(b, s), 0))
    tok_dir = pl.BlockSpec((tt, bw), lambda b, s: (row_blk(b, s), d))
    st = pl.BlockSpec((None, pairs, LANE, LANE), lambda b, s: (b, 0, 0, 0))
    c = SCAN_CHUNK
    nsub = tt // c
    group = 2 if nsub % 2 == 0 else 1
    body = functools.partial(_chunk_scan_body, reverse=reverse, nsub=nsub, pairs=pairs, group=group)
    return pl.pallas_call(
        body,
        grid=(bn, steps),
        in_specs=[tok, tok, tok, tok_dir, tok_dir, tok_dir, st],
        out_specs=[tok, st],
        out_shape=[jax.ShapeDtypeStruct((m, bw), F32), jax.ShapeDtypeStruct(s0.shape, F32)],
        scratch_shapes=[
            pltpu.VMEM((pairs, LANE, LANE), F32),
            pltpu.VMEM((nsub, pairs, 2 * c, LANE), BF16),
            pltpu.VMEM((nsub, pairs, c, LANE), F32),
            pltpu.VMEM((nsub, pairs, c, LANE), BF16),
            pltpu.VMEM((nsub, pairs, 2 * c, LANE), BF16),
            pltpu.VMEM((nsub, pairs, c, LANE), BF16),
            pltpu.VMEM((nsub, pairs, SUBLANE, LANE), F32),
        ],
        compiler_params=_cparams(("arbitrary", "arbitrary")),
    )(r, v, kk, lw, kka, kd, s0)


def _mixout_body(x_ref, gt_ref, ya_ref, yf_ref, yb_ref, bonus_ref, g_ref, lng_ref, lnb_ref, e_ref, wa_ref,
                 wb_ref, o_ref):
    e = e_ref[...]
    inv_n = 1.0 / HEAD_DIM
    y = yf_ref[...] + yb_ref[...]
    yc = y - _head_sums(y, e) * inv_n
    var = _head_sums(yc * yc, e) * inv_n
    yn = yc * lax.rsqrt(var + GN_EPS) * lng_ref[...] + lnb_ref[...]
    yb = ((yn + bonus_ref[...]) * g_ref[...]).astype(BF16)
    ox = _dot(ya_ref[...], wa_ref[...]) + _dot(yb, wb_ref[...])
    o_ref[...] = x_ref[...] + gt_ref[...] * ox


def _mix_out(x, gate, ya, y_fwd, y_bwd, bonus, g, ln_g, ln_b, e, w_out_a, w_out_b, *, rows_per_mod, tm=256):
    m, d = x.shape
    aw, bw = ya.shape[1], y_fwd.shape[1]
    tpb = rows_per_mod // tm
    full = lambda arr: pl.BlockSpec(arr.shape, lambda i: (0,) * arr.ndim)
    tok = lambda width: pl.BlockSpec((tm, width), lambda i: (i, 0))
    return pl.pallas_call(
        _mixout_body,
        grid=(m // tm,),
        in_specs=[
            tok(d), pl.BlockSpec((None, 1, d), lambda i: (i // tpb, 0, 0)),
            tok(aw), tok(bw), tok(bw), tok(bw), tok(bw),
            full(ln_g), full(ln_b), full(e), full(w_out_a), full(w_out_b),
        ],
        out_specs=tok(d),
        out_shape=jax.ShapeDtypeStruct((m, d), F32),
        compiler_params=_cparams(("arbitrary",)),
    )(x, gate, ya, y_fwd, y_bwd, bonus, g, ln_g, ln_b, e, w_out_a, w_out_b)


def kernel(x, c, ctx, c_ctx, ada_w, ada_b, norm_g, ffn_w1, ffn_w3, ffn_w2, ab_w_in, ab_w_out, gm_v_gain,
           gm_ws, gm_b, rw_mu, rw_w0, rw_w2, rw_a0, rw_a2, rw_g2, rw_k_k, rw_k_a, rw_r_k, rw_ln_g, rw_ln_b,
           sc_w_in, sc_conv, sc_w_out, final_g):
    bn, seq, d = x.shape
    ctx_len = ctx.shape[1]
    depth = ada_w.shape[0]
    n_mod = ada_w.shape[2] // d
    m_x, m_c = bn * seq, bn * ctx_len
    xs = x.reshape(m_x, d)
    cs = ctx.reshape(m_c, d)

    rows = _round_up(bn + 1, SUBLANE)
    c_rows = jnp.zeros((rows, d), F32).at[:bn].set(c).at[bn].set(c_ctx)
    mods = _modulation(c_rows, ada_w, ada_b).reshape(depth, rows, n_mod, d)

    sc_in, sc_out = sc_w_in.astype(BF16), sc_w_out.astype(BF16)

    ffn_order = [(i, half) for i in range(depth) for half in (0, 1)]
    ffn_f32 = (ffn_w1, ffn_w3, ffn_w2)
    ffn_bf16 = {ffn_order[0]: tuple(_cast_set(w, ffn_order[0]) for w in ffn_f32)}
    ffn_tm, ffn_tf = 1024, 512
    ffn_tiles = m_x // min(ffn_tm, seq)
    can_piggyback = all(_cast_plan(w.shape[-2:], d, ffn_tiles, ffn_tf) is not None for w in ffn_f32)

    def ffn(t, key, sub, mod, rows_per_mod, final=None, prepare=None):
        i = key[0]
        w1, w3, w2 = ffn_bf16[key]
        cast = ()
        if prepare is not None:
            if can_piggyback:
                cast = tuple((w, prepare) for w in ffn_f32)
            else:
                ffn_bf16[prepare] = tuple(_cast_set(w, prepare) for w in ffn_f32)
        out, copies = _glu_block(t, norm_g[i, sub], mod[3 * sub], mod[3 * sub + 1], mod[3 * sub + 2],
                                 [(w1, (), 0), (w3, (), 0)], (w2, ()), kind="swiglu", coef=0.5, final_g=final,
                                 rows_per_mod=rows_per_mod, tm=ffn_tm, tf=ffn_tf if final is None else 256,
                                 cast=cast)
        if cast:
            ffn_bf16[prepare] = tuple(copies)
        return out

    def next_key(key):
        k = ffn_order.index(key) + 1
        return ffn_order[k] if k < len(ffn_order) else None

    for i in range(depth):
        even = i % 2 == 0
        ctx_after = any(j % 2 == 0 for j in range(i + 1, depth))
        use_ctx = cs is not None and (even or ctx_after)
        if not use_ctx:
            cs = None
        mx = [mods[i, :bn, j].reshape(bn, 1, d) for j in range(n_mod)]
        mc = [mods[i, bn:bn + 1, j].reshape(1, 1, d) for j in range(n_mod)]
        last_layer = i == depth - 1

        xs = ffn(xs, (i, 0), 0, mx, seq, prepare=next_key((i, 0)))
        if cs is not None:
            cs = ffn(cs, (i, 0), 0, mc, m_c)

        if even:
            e_idx = i // 2
            if ctx_after:
                raise NotImplementedError("context output of an even layer is only needed for depth > 2")
            xs = _ab_mixer(xs, cs, mx, mc, norm_g[i, 1], bn, seq, ctx_len, ab_w_in[e_idx], ab_w_out[e_idx],
                           gm_v_gain[e_idx], gm_ws[e_idx], gm_b[e_idx], rw_mu[e_idx], rw_w0[e_idx],
                           rw_w2[e_idx], rw_a0[e_idx], rw_a2[e_idx], rw_g2[e_idx], rw_k_k[e_idx],
                           rw_k_a[e_idx], rw_r_k[e_idx], rw_ln_g[e_idx], rw_ln_b[e_idx])
        else:
            o_idx = i // 2
            if ctx_after:
                raise NotImplementedError("context output of an odd layer is only needed for depth > 2")
            ups = [(sc_in, (o_idx,), k * d) for k in range(3)]
            xs, _ = _glu_block(xs, norm_g[i, 1], mx[3], mx[4], mx[5], ups, (sc_out, (o_idx,)), kind="conv",
                               coef=1.0, conv_w=(sc_conv, (o_idx,)), rows_per_mod=seq, tm=512)
        cs = None

        xs = ffn(xs, (i, 1), 2, mx, seq, final=final_g if last_layer else None, prepare=next_key((i, 1)))
    return xs.reshape(bn, seq, d)


def _ab_mixer(xs, cs, mx, mc, g, bn, seq, ctx_len, w_in, w_out, v_gain, ws, b_s, mu, w0, w2, a0, a2, g2,
              k_k, k_a, r_k, ln_g, ln_b):
    m_x, d = xs.shape
    m_c = cs.shape[0]
    bw = k_k.shape[0]
    aw = v_gain.shape[0]
    a_cols = 2 * aw
    heads = bw // HEAD_DIM
    wl, al, gl = w2.shape[1], a2.shape[1], g2.shape[0]
    b_cols = w_in.shape[1] - a_cols
    pw = _round_up(b_cols, 512)
    gp = pw - (3 * bw + 2 * wl + 2 * al)

    w_in_bf = w_in.astype(BF16)
    w_in_a = w_in_bf[:, :a_cols]
    w_in_b = jnp.pad(w_in_bf[:, a_cols:], ((0, 0), (0, pw - b_cols)))

    mu_p = jnp.pad(mu, ((0, 0), (0, pw - b_cols)))
    zeros_w = jnp.zeros((wl, bw), F32)
    zeros_a = jnp.zeros((al, bw), F32)
    w2cat = jnp.concatenate([jnp.concatenate([w2[0], zeros_w], 1), jnp.concatenate([zeros_w, w2[1]], 1)], 0)
    a2cat = jnp.concatenate([jnp.concatenate([a2[0], zeros_a], 1), jnp.concatenate([zeros_a, a2[1]], 1)], 0)
    g2p = jnp.pad(g2, ((0, gp - gl), (0, 0)))
    head_id = jnp.arange(min(bw, MXU_WIDTH)) // HEAD_DIM
    e = (head_id[:, None] == head_id[None, :]).astype(BF16)
    consts = (mu_p, w0.reshape(1, 2 * bw), w2cat.astype(BF16), a0.reshape(1, 2 * bw), a2cat.astype(BF16),
              g2p.astype(BF16), k_k.reshape(1, bw), k_a.reshape(1, bw), r_k.reshape(1, bw), e)

    b_full = jnp.repeat(b_s.T, aw // ws.shape[0], axis=1)
    ya = _gmlp_branch(xs, g, mx[3], mx[4], w_in_a, v_gain, ws.astype(BF16), b_full, rows_per_mod=seq)

    fx = _rwkv_front(xs, g, mx[3], mx[4], w_in_b, seq, consts, rows_per_mod=seq)
    fc = _rwkv_front(cs, g, mc[3], mc[4], w_in_b, ctx_len, consts, rows_per_mod=m_c)
    r_x, v_x, kk_x, g_x, bonus_x, lw_x, kka_x, kd_x = fx
    r_c, v_c, kk_c, _, _, lw_c, kka_c, kd_c = fc
    s_zero = jnp.zeros((bn, bw // LANE, LANE, LANE), F32)
    ys = []
    for reverse in (False, True):
        _, s_ctx = _chunk_scan(r_c, v_c, kk_c, lw_c, kka_c, kd_c, s_zero, bn=bn, reverse=reverse)
        y_dir, _ = _chunk_scan(r_x, v_x, kk_x, lw_x, kka_x, kd_x, s_ctx, bn=bn, reverse=reverse)
        ys.append(y_dir)

    w_out_bf = w_out.astype(BF16)
    return _mix_out(xs, mx[5], ya, ys[0], ys[1], bonus_x, g_x, ln_g.reshape(1, bw), ln_b.reshape(1, bw), e,
                    w_out_bf[:aw], w_out_bf[aw:], rows_per_mod=seq)
```

```python
import functools
import math

import jax
import jax.numpy as jnp
from jax import lax
from jax.experimental import pallas as pl
from jax.experimental.pallas import tpu as pltpu

F32 = jnp.float32
BF16 = jnp.bfloat16

GRID_W = 64
CHUNK = 128
HEAD_DIM = 64
RMS_EPS = 1e-6
LN_EPS = 1e-5
GN_EPS = 64e-5
KK_EPS = 1e-12
W_DECAY_SCALE = math.exp(-0.5)

LANE = 128
SUBLANE = 8
MXU_WIDTH = 256
VMEM_LIMIT = 58 * 1024 * 1024


def _cparams(sem, limit=None):
    return pltpu.CompilerParams(dimension_semantics=sem, vmem_limit_bytes=VMEM_LIMIT if limit is None else limit)


def _round_up(n, m):
    return (n + m - 1) // m * m


def _dot(a, b):
    return jnp.dot(a, b, preferred_element_type=F32)


def _dot2(a, b):
    hi = a.astype(BF16)
    lo = (a - hi.astype(F32)).astype(BF16)
    return _dot(hi, b) + _dot(lo, b)


def _head_sums(x, e):
    blk = e.shape[0]
    parts = [_dot2(x[:, j:j + blk], e) for j in range(0, x.shape[1], blk)]
    return parts[0] if len(parts) == 1 else jnp.concatenate(parts, axis=1)


ROW_BLOCK = 16
ROW_UNROLL = 8
EPILOGUE_ROWS = 128


def _prenorm_rows(x_ref, g, shift, scale, h_ref, *, src_row=0, dst_row=0, n_rows=None, zero_ref=None):
    n_rows = x_ref.shape[0] if n_rows is None else n_rows
    gs = g * (1.0 + scale)

    def block(i, carry):
        off = pl.multiple_of(i * ROW_BLOCK, ROW_BLOCK)
        x = x_ref[pl.ds(src_row + off, ROW_BLOCK), :]
        ms = jnp.mean(x * x, axis=-1, keepdims=True)
        h_ref[pl.ds(dst_row + off, ROW_BLOCK), :] = (x * lax.rsqrt(ms + RMS_EPS) * gs + shift).astype(h_ref.dtype)
        if zero_ref is not None:
            zero_ref[pl.ds(dst_row + off, ROW_BLOCK), :] = jnp.zeros((ROW_BLOCK, zero_ref.shape[1]), zero_ref.dtype)
        return carry

    n_blocks = n_rows // ROW_BLOCK
    lax.fori_loop(0, n_blocks, block, 0, unroll=math.gcd(n_blocks, ROW_UNROLL))


def _mod_body(c_ref, w_ref, b_ref, o_ref):
    s = jax.nn.silu(c_ref[...]).astype(BF16)
    o_ref[...] = _dot(s, w_ref[...].astype(BF16)) + b_ref[...]


def _modulation(c_rows, ada_w, ada_b):
    depth, d, n = ada_w.shape
    tn = 1024 if n % 1024 == 0 else 512
    rows = c_rows.shape[0]
    return pl.pallas_call(
        _mod_body,
        grid=(depth, n // tn),
        in_specs=[
            pl.BlockSpec((rows, d), lambda l, j: (0, 0)),
            pl.BlockSpec((None, d, tn), lambda l, j: (l, 0, j)),
            pl.BlockSpec((None, 1, tn), lambda l, j: (l, 0, j)),
        ],
        out_specs=pl.BlockSpec((None, rows, tn), lambda l, j: (l, 0, j)),
        out_shape=jax.ShapeDtypeStruct((depth, rows, n), F32),
        compiler_params=_cparams(("arbitrary", "arbitrary")),
    )(c_rows, ada_w, ada_b.reshape(depth, 1, n))


def _glu_body(*refs, n_up, kind, coef, final, nf, n_cast):
    x_ref, g_ref, sh_ref, sc_ref, gt_ref = refs[:5]
    up_refs = refs[5:5 + n_up]
    down_ref = refs[5 + n_up]
    idx = 6 + n_up
    cw_ref = fg_ref = None
    if kind == "conv":
        cw_ref = refs[idx]
        idx += 1
    if final:
        fg_ref = refs[idx]
        idx += 1
    cast_in = refs[idx:idx + n_cast]
    idx += n_cast
    o_ref = refs[idx]
    cast_out = refs[idx + 1:idx + 1 + n_cast]
    h_ref = refs[idx + 1 + n_cast]
    f = pl.program_id(1)

    for ci, co in zip(cast_in, cast_out):
        co[...] = ci[...].astype(co.dtype)

    @pl.when(f == 0)
    def _():
        _prenorm_rows(x_ref, g_ref[...], sh_ref[...], sc_ref[...], h_ref, zero_ref=o_ref)

    h = h_ref[...]
    ups = [_dot(h, u[...]) for u in up_refs]
    if kind == "swiglu":
        mid = jax.nn.silu(ups[0]) * ups[1]
    else:
        z = ups[1] * ups[2]
        rows = z.shape[0]
        col = lax.broadcasted_iota(jnp.int32, z.shape, 0) % GRID_W
        zp = jnp.where(col == 0, 0.0, pltpu.roll(z, 1, axis=0))
        zn = jnp.where(col == GRID_W - 1, 0.0, pltpu.roll(z, rows - 1, axis=0))
        cw = cw_ref[...]
        mid = ups[0] * (cw[0:1] * zp + cw[1:2] * z + cw[2:3] * zn)
    o_ref[...] += _dot(mid.astype(BF16), down_ref[...])

    @pl.when(f == nf - 1)
    def _():
        cg = coef * gt_ref[...]
        tm = o_ref.shape[0]
        step = min(tm, EPILOGUE_ROWS)
        for r0 in range(0, tm, step):
            rows = slice(r0, r0 + step)
            res = x_ref[rows, :] + cg * o_ref[rows, :]
            if final:
                ms = jnp.mean(res * res, axis=-1, keepdims=True)
                res = res * lax.rsqrt(ms + RMS_EPS) * fg_ref[...]
            o_ref[rows, :] = res


def _cast_plan(mat, d, n_tiles, tf):
    if d % n_tiles or (d // n_tiles) % LANE:
        return None
    if mat[0] == d:
        return (d // n_tiles, tf), (lambda i, f: (i, f))
    return (tf, d // n_tiles), (lambda i, f: (f, i))


def _glu_block(x, g, shift, scale, gate, ups, down, *, kind, coef, conv_w=None, final_g=None,
               rows_per_mod, tm=1024, tf=512, cast=(), vmem_limit=None):
    m, d = x.shape
    down_w, down_lead = down
    fdim = down_w.shape[-2]
    tm = min(tm, m, rows_per_mod)
    nf = fdim // tf
    tpb = rows_per_mod // tm
    mod_spec = pl.BlockSpec((None, 1, d), lambda i, f: (i // tpb, 0, 0))
    in_specs = [
        pl.BlockSpec((tm, d), lambda i, f: (i, 0)),
        pl.BlockSpec((1, d), lambda i, f: (0, 0)),
        mod_spec, mod_spec, mod_spec,
    ]
    args = [x, g.reshape(1, d), shift, scale, gate]
    for w, lead, col0 in ups:
        blk0 = col0 // tf
        in_specs.append(pl.BlockSpec((None,) * len(lead) + (d, tf),
                                     lambda i, f, lead=lead, blk0=blk0: lead + (0, blk0 + f)))
        args.append(w)
    in_specs.append(pl.BlockSpec((None,) * len(down_lead) + (tf, d), lambda i, f: down_lead + (f, 0)))
    args.append(down_w)
    if kind == "conv":
        cw, cw_lead = conv_w
        in_specs.append(pl.BlockSpec((None,) * len(cw_lead) + (3, tf), lambda i, f: cw_lead + (0, f)))
        args.append(cw)
    if final_g is not None:
        in_specs.append(pl.BlockSpec((1, d), lambda i, f: (0, 0)))
        args.append(final_g.reshape(1, d))
    steps = (m // tm) * nf
    out_specs = [pl.BlockSpec((tm, d), lambda i, f: (i, 0))]
    out_shape = [jax.ShapeDtypeStruct((m, d), F32)]
    for w, lead in cast:
        blk, order = _cast_plan(w.shape[-2:], d, m // tm, tf)
        in_specs.append(pl.BlockSpec((None,) * len(lead) + blk,
                                     lambda i, f, lead=lead, order=order: lead + order(i, f)))
        args.append(w)
        out_specs.append(pl.BlockSpec(blk, order))
        out_shape.append(jax.ShapeDtypeStruct(w.shape[-2:], BF16))
    body = functools.partial(_glu_body, n_up=len(ups), kind=kind, coef=coef,
                             final=final_g is not None, nf=nf, n_cast=len(cast))
    res = pl.pallas_call(
        body,
        grid=(m // tm, nf),
        in_specs=in_specs,
        out_specs=out_specs,
        out_shape=out_shape,
        scratch_shapes=[pltpu.VMEM((tm, d), BF16)],
        compiler_params=_cparams(("arbitrary", "arbitrary"), vmem_limit),
    )(*args)
    return res[0], list(res[1:])


def _cast_body(i_ref, o_ref):
    o_ref[...] = i_ref[...].astype(o_ref.dtype)


def _cast_set(w, lead, rows=512):
    n_rows, n_cols = w.shape[-2:]
    rows = math.gcd(rows, n_rows)
    return pl.pallas_call(
        _cast_body,
        grid=(n_rows // rows,),
        in_specs=[pl.BlockSpec((None,) * len(lead) + (rows, n_cols), lambda i: lead + (i, 0))],
        out_specs=pl.BlockSpec((rows, n_cols), lambda i: (i, 0)),
        out_shape=jax.ShapeDtypeStruct((n_rows, n_cols), BF16),
        compiler_params=_cparams(("arbitrary",)),
    )(w)


def _gmlp_body(x_ref, g_ref, sh_ref, sc_ref, w_ref, gain_ref, ws_ref, b_ref, o_ref, h_ref, p_ref, *,
               aw, groups, n_chunks):
    _prenorm_rows(x_ref, g_ref[...], sh_ref[...], sc_ref[...], h_ref)
    p_ref[...] = _dot(h_ref[...], w_ref[...])
    gd = aw // groups
    for c in range(n_chunks):
        rows = pl.ds(c * CHUNK, CHUNK)
        u = jax.nn.gelu(p_ref[rows, 0:aw])
        v = jax.nn.gelu(p_ref[rows, aw:2 * aw])
        vm = jnp.mean(v, axis=-1, keepdims=True)
        vc = v - vm
        vv = jnp.mean(vc * vc, axis=-1, keepdims=True)
        vn = (vc * lax.rsqrt(vv + LN_EPS) * gain_ref[...]).astype(BF16)
        for gi in range(groups):
            cols = slice(gi * gd, (gi + 1) * gd)
            vs = _dot(ws_ref[gi], vn[:, cols]) + b_ref[:, cols]
            o_ref[rows, cols] = (u[:, cols] * vs).astype(o_ref.dtype)


def _gmlp_branch(x, g, shift, scale, w_a, gain, ws, b_full, *, rows_per_mod, tm=512):
    m, d = x.shape
    two_aw = w_a.shape[1]
    aw = two_aw // 2
    groups = ws.shape[0]
    tm = min(tm, m, rows_per_mod)
    tpb = rows_per_mod // tm
    mod_spec = pl.BlockSpec((None, 1, d), lambda i: (i // tpb, 0, 0))
    body = functools.partial(_gmlp_body, aw=aw, groups=groups, n_chunks=tm // CHUNK)
    return pl.pallas_call(
        body,
        grid=(m // tm,),
        in_specs=[
            pl.BlockSpec((tm, d), lambda i: (i, 0)),
            pl.BlockSpec((1, d), lambda i: (0, 0)),
            mod_spec, mod_spec,
            pl.BlockSpec((d, two_aw), lambda i: (0, 0), pipeline_mode=pl.Buffered(1)),
            pl.BlockSpec((1, aw), lambda i: (0, 0)),
            pl.BlockSpec((groups, CHUNK, CHUNK), lambda i: (0, 0, 0)),
            pl.BlockSpec((CHUNK, aw), lambda i: (0, 0)),
        ],
        out_specs=pl.BlockSpec((tm, aw), lambda i: (i, 0)),
        out_shape=jax.ShapeDtypeStruct((m, aw), BF16),
        scratch_shapes=[pltpu.VMEM((tm, d), BF16), pltpu.VMEM((tm, two_aw), F32)],
        compiler_params=_cparams(("arbitrary",)),
    )(x, g.reshape(1, d), shift, scale, w_a, gain.reshape(1, aw), ws, b_full)


def _prep_body(x_ref, xp_ref, xn_ref, g_ref, sh_ref, sc_ref, w_ref, mu_ref, w0_ref, w2_ref, a0_ref, a2_ref,
               g2_ref, kk_ref, ka_ref, rk_ref, e_ref, r_o, v_o, kk_o, g_o, bonus_o, lw_o, kka_o, kd_o,
               h_ref, p_ref, *, bw, wl2, al2, seq, tm):
    i = pl.program_id(0)
    g, sh, sc = g_ref[...], sh_ref[...], sc_ref[...]
    _prenorm_rows(x_ref, g, sh, sc, h_ref)
    _prenorm_rows(xp_ref, g, sh, sc, h_ref, dst_row=tm)
    _prenorm_rows(xn_ref, g, sh, sc, h_ref, dst_row=tm + ROW_BLOCK)
    p_ref[...] = _dot(h_ref[...], w_ref[...])

    p = p_ref[0:tm, :]
    row = lax.broadcasted_iota(jnp.int32, p.shape, 0)
    first = (i * tm) % seq == 0
    last = ((i + 1) * tm) % seq == 0
    prev_row = jnp.where(first, 0.0, p_ref[tm + ROW_BLOCK - 1:tm + ROW_BLOCK, :])
    next_row = jnp.where(last, 0.0, p_ref[tm + ROW_BLOCK:tm + ROW_BLOCK + 1, :])
    prev = jnp.where(row == 0, prev_row, pltpu.roll(p, 1, axis=0))
    nxt = jnp.where(row == tm - 1, next_row, pltpu.roll(p, tm - 1, axis=0))
    mu = mu_ref[...]
    p = p + mu[0:1] * (prev - p) + mu[1:2] * (nxt - p)

    r = p[:, 0:bw]
    k = p[:, bw:2 * bw]
    v = p[:, 2 * bw:3 * bw]
    o = 3 * bw
    wd = jnp.tanh(p[:, o:o + wl2]).astype(BF16)
    ad = p[:, o + wl2:o + wl2 + al2].astype(BF16)
    gdn = jax.nn.sigmoid(p[:, o + wl2 + al2:]).astype(BF16)

    lw = -W_DECAY_SCALE * jax.nn.sigmoid(w0_ref[...] + _dot(wd, w2_ref[...]))
    a = jax.nn.sigmoid(a0_ref[...] + _dot(ad, a2_ref[...]))
    gate = _dot(gdn, g2_ref[...])

    e = e_ref[...]
    kq = k * kk_ref[...]
    kk = kq * lax.rsqrt(_head_sums(kq * kq, e) + KK_EPS)
    ka = ka_ref[...]
    kd0 = k * (1.0 + (a[:, 0:bw] - 1.0) * ka)
    kd1 = k * (1.0 + (a[:, bw:] - 1.0) * ka)
    kb = 0.5 * (kd0 + kd1)
    bonus = _head_sums(r * kb * rk_ref[...], e) * v

    r_o[...] = r
    v_o[...] = v
    kk_o[...] = kk
    g_o[...] = gate
    bonus_o[...] = bonus
    lw_o[...] = lw
    kka_o[:, 0:bw] = kk * a[:, 0:bw]
    kka_o[:, bw:] = kk * a[:, bw:]
    kd_o[:, 0:bw] = kd0
    kd_o[:, bw:] = kd1


def _rwkv_front(x, g, shift, scale, w_b, seq, consts, *, rows_per_mod, tm=256):
    m, d = x.shape
    pw = w_b.shape[1]
    (mu, w0, w2cat, a0, a2cat, g2p, k_k, k_a, r_k, e) = consts
    bw = k_k.shape[1]
    wl2, al2 = w2cat.shape[0], a2cat.shape[0]
    tm = min(tm, seq)
    tpb = rows_per_mod // tm
    nblk = tm // ROW_BLOCK
    last_blk = m // ROW_BLOCK - 1
    full = lambda arr: pl.BlockSpec(arr.shape, lambda i: (0,) * arr.ndim)
    mod_spec = pl.BlockSpec((None, 1, d), lambda i: (i // tpb, 0, 0))
    body = functools.partial(_prep_body, bw=bw, wl2=wl2, al2=al2, seq=seq, tm=tm)
    one = jax.ShapeDtypeStruct((m, bw), F32)
    two = jax.ShapeDtypeStruct((m, 2 * bw), F32)
    o1 = pl.BlockSpec((tm, bw), lambda i: (i, 0))
    o2 = pl.BlockSpec((tm, 2 * bw), lambda i: (i, 0))
    return pl.pallas_call(
        body,
        grid=(m // tm,),
        in_specs=[
            pl.BlockSpec((tm, d), lambda i: (i, 0)),
            pl.BlockSpec((ROW_BLOCK, d), lambda i: (jnp.maximum(i * nblk - 1, 0), 0)),
            pl.BlockSpec((ROW_BLOCK, d), lambda i: (jnp.minimum((i + 1) * nblk, last_blk), 0)),
            pl.BlockSpec((1, d), lambda i: (0, 0)),
            mod_spec, mod_spec,
            pl.BlockSpec((d, pw), lambda i: (0, 0), pipeline_mode=pl.Buffered(1)),
            full(mu), full(w0), full(w2cat), full(a0), full(a2cat), full(g2p), full(k_k), full(k_a),
            full(r_k), full(e),
        ],
        out_specs=[o1, o1, o1, o1, o1, o2, o2, o2],
        out_shape=[one, one, one, one, one, two, two, two],
        scratch_shapes=[pltpu.VMEM((tm + 2 * ROW_BLOCK, d), BF16), pltpu.VMEM((tm + 2 * ROW_BLOCK, pw), F32)],
        compiler_params=_cparams(("arbitrary",)),
    )(x, x, x, g.reshape(1, d), shift, scale, w_b, mu, w0, w2cat, a0, a2cat, g2p, k_k, k_a, r_k, e)


SCAN_CHUNK = 64
INV_BLOCK = 16


def _dot_nt(a, b):
    return lax.dot_general(a, b, (((1,), (1,)), ((), ())), preferred_element_type=F32)


def _dot_tn(a, b):
    return lax.dot_general(a, b, (((0,), (0,)), ((), ())), preferred_element_type=F32)


def _pair_blockdiag(x, lane_lo):
    return jnp.concatenate([jnp.where(lane_lo, x, 0.0), jnp.where(lane_lo, 0.0, x)], axis=0).astype(BF16)


def _chunk_scan_body(r_ref, v_ref, kk_ref, lw_ref, kka_ref, kd_ref, s0_ref, y_ref, sout_ref,
                     s_ref, qw_ref, u0_ref, arb_ref, pe_ref, vb_ref, elt_ref, *, reverse, nsub, pairs, group):
    c = SCAN_CHUNK
    step = pl.program_id(1)
    bf = lambda t: t.astype(BF16)

    @pl.when(step == 0)
    def _():
        s_ref[...] = s0_ref[...]

    rp = lax.broadcasted_iota(jnp.int32, (c, LANE), 0)
    cp = lax.broadcasted_iota(jnp.int32, (c, LANE), 1) % c
    incl_p = (rp <= cp) if reverse else (rp >= cp)
    strict_p = (rp < cp) if reverse else (rp > cp)
    r2 = lax.broadcasted_iota(jnp.int32, (2 * c, LANE), 0)
    c2 = lax.broadcasted_iota(jnp.int32, (2 * c, LANE), 1) % c
    t2 = r2 % c
    mask_kr = ((t2 < c2) if reverse else (t2 > c2)) | ((r2 >= c) & (t2 == c2))
    blk_p = (rp // INV_BLOCK) == (cp // INV_BLOCK)
    eye_p = (rp == cp).astype(F32)
    lane_lo = lax.broadcasted_iota(jnp.int32, (c, LANE), 1) < HEAD_DIM
    bi = lax.broadcasted_iota(jnp.int32, (LANE, LANE), 0) // HEAD_DIM
    bj = lax.broadcasted_iota(jnp.int32, (LANE, LANE), 1) // HEAD_DIM
    same_head = bi == bj
    bd = lambda t: _pair_blockdiag(t, lane_lo)
    prange = range(pairs)
    lanes = [pl.ds(p * LANE, LANE) for p in prange]

    def phase_a(chunks):
        streams = [(j, p) for j in chunks for p in prange]
        sr = range(len(streams))
        rows = [pl.ds(j * c, c) for j, _ in streams]
        cols = [lanes[p] for _, p in streams]
        lw = [lw_ref[rows[s], cols[s]] for s in sr]
        lc = list(lw)
        shift = 1
        while shift < c:
            for s in sr:
                if reverse:
                    moved = jnp.where(rp < c - shift, pltpu.roll(lc[s], c - shift, axis=0), 0.0)
                else:
                    moved = jnp.where(rp >= shift, pltpu.roll(lc[s], shift, axis=0), 0.0)
                lc[s] = lc[s] + moved
            shift *= 2
        yield
        qq, qk_bd, qr, pk_bd, pb_bd, v_bd = [], [], [], [], [], []
        for s, (j, p) in enumerate(streams):
            ltot = lc[s][0:1] if reverse else lc[s][c - 1:c]
            e_neg = jnp.exp(-lc[s])
            e_end = jnp.exp(ltot - lc[s])
            kd = kd_ref[rows[s], cols[s]]
            kka = kka_ref[rows[s], cols[s]]
            v = v_ref[rows[s], cols[s]]
            qk = kk_ref[rows[s], cols[s]] * jnp.exp(lc[s] - lw[s])
            qr.append(r_ref[rows[s], cols[s]] * jnp.exp(lc[s]))
            qq.append(bf(jnp.concatenate([qk, qr[s]], axis=0)))
            qk_bd.append(bd(qk))
            pk_bd.append(bd(kd * e_neg))
            pb_bd.append(bd(kka * e_neg))
            v_bd.append(bd(v))
            pe_ref[j, p] = bf(jnp.concatenate([kd * e_end, -(kka * e_end)], axis=0))
            vb_ref[j, p] = bf(v)
            elt_ref[j, p] = jnp.broadcast_to(jnp.exp(ltot), (SUBLANE, LANE))
        g = [_dot_nt(qq[s], jnp.concatenate([pk_bd[s], pb_bd[s]], axis=0)) for s in sr]
        yield
        a_k = [bf(jnp.where(mask_kr, g[s][:, :LANE], 0.0)) for s in sr]
        nmat = [jnp.where(strict_p, g[s][:c, LANE:], 0.0) for s in sr]
        for s, (j, p) in enumerate(streams):
            arb_ref[j, p] = bf(jnp.where(incl_p, g[s][c:, LANE:], 0.0))
        avy = [_dot(a_k[s], v_bd[s]) for s in sr]
        av = [avy[s][:c] for s in sr]
        for s in sr:
            y_ref[rows[s], cols[s]] = avy[s][c:]
        nd = [jnp.where(blk_p, nmat[s], 0.0) for s in sr]
        lo_bd = [bd(nmat[s] - nd[s]) for s in sr]
        dinv = [eye_p - nd[s] for s in sr]
        mpow = [_dot(bf(-nd[s]), bd(-nd[s])) for s in sr]
        yield
        for _ in range(int(math.log2(INV_BLOCK)) - 2):
            both = [_dot(bf(jnp.concatenate([mpow[s], dinv[s]], axis=0)), bd(mpow[s])) for s in sr]
            mpow = [both[s][:c] for s in sr]
            dinv = [dinv[s] + both[s][c:] for s in sr]
            yield
        dinv = [dinv[s] + _dot(bf(dinv[s]), bd(mpow[s])) for s in sr]
        yield
        dinv_b = [bf(dinv[s]) for s in sr]
        x1 = [_dot(dinv_b[s], lo_bd[s]) for s in sr]
        yield
        x1b = [bf(x1[s]) for s in sr]
        acc = [eye_p - x1[s] for s in sr]
        xp = x1
        for k in range(2, SCAN_CHUNK // INV_BLOCK):
            xp = [_dot(x1b[s], bd(xp[s])) for s in sr]
            yield
            acc = [acc[s] + xp[s] if k % 2 == 0 else acc[s] - xp[s] for s in sr]
        tinv = [bf(_dot(bf(acc[s]), bd(dinv[s]))) for s in sr]
        yield
        wt = [_dot(tinv[s], qk_bd[s]) for s in sr]
        for s, (j, p) in enumerate(streams):
            u0_ref[j, p] = _dot(tinv[s], bd(av[s]))
            qw_ref[j, p] = bf(jnp.concatenate([wt[s], qr[s]], axis=0))

    def phase_b(chunks):
        for jj in chunks:
            rows = pl.ds(jj * c, c)
            s = [s_ref[p] for p in prange]
            xs = [_dot_nt(qw_ref[jj, p], bf(s[p])) for p in prange]
            yield
            u = [u0_ref[jj, p] + xs[p][:c] for p in prange]
            au = [_dot(arb_ref[jj, p], bd(u[p])) for p in prange]
            upd = [_dot_tn(jnp.concatenate([vb_ref[jj, p], bf(u[p])], axis=0), pe_ref[jj, p]) for p in prange]
            yield
            for p in prange:
                y_ref[rows, lanes[p]] = y_ref[rows, lanes[p]] + xs[p][c:] - au[p]
                s_ref[p] = s[p] * elt_ref[jj, p][0:1] + jnp.where(same_head, upd[p], 0.0)
            yield

    order = list(range(nsub))[::-1] if reverse else list(range(nsub))
    groups = [order[i:i + group] for i in range(0, nsub, group)]
    pending = None
    for chunks in groups + [None]:
        live = [gen for gen in (phase_a(chunks) if chunks is not None else None, pending) if gen is not None]
        while live:
            for gen in list(live):
                try:
                    next(gen)
                except StopIteration:
                    live.remove(gen)
        pending = phase_b(chunks) if chunks is not None else None

    @pl.when(step == pl.num_programs(1) - 1)
    def _():
        sout_ref[...] = s_ref[...]


def _chunk_scan(r, v, kk, lw, kka, kd, s0, *, bn, reverse, tt=512):
    m, bw = r.shape
    seq = m // bn
    tt = min(tt, seq)
    steps = seq // tt
    pairs = bw // LANE
    d = 1 if reverse else 0

    def row_blk(b, s):
        return b * steps + (steps - 1 - s if reverse else s)

    tok = pl.BlockSpec((tt, bw), lambda b, s: (row_blk(b, s), 0))
    tok_dir = pl.BlockSpec((tt, bw), lambda b, s: (row_blk(b, s), d))
    st = pl.BlockSpec((None, pairs, LANE, LANE), lambda b, s: (b, 0, 0, 0))
    c = SCAN_CHUNK
    nsub = tt // c
    group = 2 if nsub % 2 == 0 else 1
    body = functools.partial(_chunk_scan_body, reverse=reverse, nsub=nsub, pairs=pairs, group=group)
    return pl.pallas_call(
        body,
        grid=(bn, steps),
        in_specs=[tok, tok, tok, tok_dir, tok_dir, tok_dir, st],
        out_specs=[tok, st],
        out_shape=[jax.ShapeDtypeStruct((m, bw), F32), jax.ShapeDtypeStruct(s0.shape, F32)],
        scratch_shapes=[
            pltpu.VMEM((pairs, LANE, LANE), F32),
            pltpu.VMEM((nsub, pairs, 2 * c, LANE), BF16),
            pltpu.VMEM((nsub, pairs, c, LANE), F32),
            pltpu.VMEM((nsub, pairs, c, LANE), BF16),
            pltpu.VMEM((nsub, pairs, 2 * c, LANE), BF16),
            pltpu.VMEM((nsub, pairs, c, LANE), BF16),
            pltpu.VMEM((nsub, pairs, SUBLANE, LANE), F32),
        ],
        compiler_params=_cparams(("arbitrary", "arbitrary")),
    )(r, v, kk, lw, kka, kd, s0)


def _mixout_body(x_ref, gt_ref, ya_ref, yf_ref, yb_ref, bonus_ref, g_ref, lng_ref, lnb_ref, e_ref, wa_ref,
                 wb_ref, o_ref):
    e = e_ref[...]
    inv_n = 1.0 / HEAD_DIM
    y = yf_ref[...] + yb_ref[...]
    yc = y - _head_sums(y, e) * inv_n
    var = _head_sums(yc * yc, e) * inv_n
    yn = yc * lax.rsqrt(var + GN_EPS) * lng_ref[...] + lnb_ref[...]
    yb = ((yn + bonus_ref[...]) * g_ref[...]).astype(BF16)
    ox = _dot(ya_ref[...], wa_ref[...]) + _dot(yb, wb_ref[...])
    o_ref[...] = x_ref[...] + gt_ref[...] * ox


def _mix_out(x, gate, ya, y_fwd, y_bwd, bonus, g, ln_g, ln_b, e, w_out_a, w_out_b, *, rows_per_mod, tm=256):
    m, d = x.shape
    aw, bw = ya.shape[1], y_fwd.shape[1]
    tpb = rows_per_mod // tm
    full = lambda arr: pl.BlockSpec(arr.shape, lambda i: (0,) * arr.ndim)
    tok = lambda width: pl.BlockSpec((tm, width), lambda i: (i, 0))
    return pl.pallas_call(
        _mixout_body,
        grid=(m // tm,),
        in_specs=[
            tok(d), pl.BlockSpec((None, 1, d), lambda i: (i // tpb, 0, 0)),
            tok(aw), tok(bw), tok(bw), tok(bw), tok(bw),
            full(ln_g), full(ln_b), full(e), full(w_out_a), full(w_out_b),
        ],
        out_specs=tok(d),
        out_shape=jax.ShapeDtypeStruct((m, d), F32),
        compiler_params=_cparams(("arbitrary",)),
    )(x, gate, ya, y_fwd, y_bwd, bonus, g, ln_g, ln_b, e, w_out_a, w_out_b)


def kernel(x, c, ctx, c_ctx, ada_w, ada_b, norm_g, ffn_w1, ffn_w3, ffn_w2, ab_w_in, ab_w_out, gm_v_gain,
           gm_ws, gm_b, rw_mu, rw_w0, rw_w2, rw_a0, rw_a2, rw_g2, rw_k_k, rw_k_a, rw_r_k, rw_ln_g, rw_ln_b,
           sc_w_in, sc_conv, sc_w_out, final_g):
    bn, seq, d = x.shape
    ctx_len = ctx.shape[1]
    depth = ada_w.shape[0]
    n_mod = ada_w.shape[2] // d
    m_x, m_c = bn * seq, bn * ctx_len
    xs = x.reshape(m_x, d)
    cs = ctx.reshape(m_c, d)

    rows = _round_up(bn + 1, SUBLANE)
    c_rows = jnp.zeros((rows, d), F32).at[:bn].set(c).at[bn].set(c_ctx)
    mods = _modulation(c_rows, ada_w, ada_b).reshape(depth, rows, n_mod, d)

    sc_in, sc_out = sc_w_in.astype(BF16), sc_w_out.astype(BF16)

    ffn_order = [(i, half) for i in range(depth) for half in (0, 1)]
    ffn_f32 = (ffn_w1, ffn_w3, ffn_w2)
    ffn_bf16 = {ffn_order[0]: tuple(_cast_set(w, ffn_order[0]) for w in ffn_f32)}
    ffn_tm, ffn_tf = 1024, 512
    ffn_tiles = m_x // min(ffn_tm, seq)
    can_piggyback = all(_cast_plan(w.shape[-2:], d, ffn_tiles, ffn_tf) is not None for w in ffn_f32)

    def ffn(t, key, sub, mod, rows_per_mod, final=None, prepare=None):
        i = key[0]
        w1, w3, w2 = ffn_bf16[key]
        cast = ()
        if prepare is not None:
            if can_piggyback:
                cast = tuple((w, prepare) for w in ffn_f32)
            else:
                ffn_bf16[prepare] = tuple(_cast_set(w, prepare) for w in ffn_f32)
        out, copies = _glu_block(t, norm_g[i, sub], mod[3 * sub], mod[3 * sub + 1], mod[3 * sub + 2],
                                 [(w1, (), 0), (w3, (), 0)], (w2, ()), kind="swiglu", coef=0.5, final_g=final,
                                 rows_per_mod=rows_per_mod, tm=ffn_tm, tf=ffn_tf if final is None else 256,
                                 cast=cast)
        if cast:
            ffn_bf16[prepare] = tuple(copies)
        return out

    def next_key(key):
        k = ffn_order.index(key) + 1
        return ffn_order[k] if k < len(ffn_order) else None

    for i in range(depth):
        even = i % 2 == 0
        ctx_after = any(j % 2 == 0 for j in range(i + 1, depth))
        use_ctx = cs is not None and (even or ctx_after)
        if not use_ctx:
            cs = None
        mx = [mods[i, :bn, j].reshape(bn, 1, d) for j in range(n_mod)]
        mc = [mods[i, bn:bn + 1, j].reshape(1, 1, d) for j in range(n_mod)]
        last_layer = i == depth - 1

        xs = ffn(xs, (i, 0), 0, mx, seq, prepare=next_key((i, 0)))
        if cs is not None:
            cs = ffn(cs, (i, 0), 0, mc, m_c)

        if even:
            e_idx = i // 2
            if ctx_after:
                raise NotImplementedError("context output of an even layer is only needed for depth > 2")
            xs = _ab_mixer(xs, cs, mx, mc, norm_g[i, 1], bn, seq, ctx_len, ab_w_in[e_idx], ab_w_out[e_idx],
                           gm_v_gain[e_idx], gm_ws[e_idx], gm_b[e_idx], rw_mu[e_idx], rw_w0[e_idx],
                           rw_w2[e_idx], rw_a0[e_idx], rw_a2[e_idx], rw_g2[e_idx], rw_k_k[e_idx],
                           rw_k_a[e_idx], rw_r_k[e_idx], rw_ln_g[e_idx], rw_ln_b[e_idx])
        else:
            o_idx = i // 2
            if ctx_after:
                raise NotImplementedError("context output of an odd layer is only needed for depth > 2")
            ups = [(sc_in, (o_idx,), k * d) for k in range(3)]
            xs, _ = _glu_block(xs, norm_g[i, 1], mx[3], mx[4], mx[5], ups, (sc_out, (o_idx,)), kind="conv",
                               coef=1.0, conv_w=(sc_conv, (o_idx,)), rows_per_mod=seq, tm=512)
        cs = None

        xs = ffn(xs, (i, 1), 2, mx, seq, final=final_g if last_layer else None, prepare=next_key((i, 1)))
    return xs.reshape(bn, seq, d)


def _ab_mixer(xs, cs, mx, mc, g, bn, seq, ctx_len, w_in, w_out, v_gain, ws, b_s, mu, w0, w2, a0, a2, g2,
              k_k, k_a, r_k, ln_g, ln_b):
    m_x, d = xs.shape
    m_c = cs.shape[0]
    bw = k_k.shape[0]
    aw = v_gain.shape[0]
    a_cols = 2 * aw
    heads = bw // HEAD_DIM
    wl, al, gl = w2.shape[1], a2.shape[1], g2.shape[0]
    b_cols = w_in.shape[1] - a_cols
    pw = _round_up(b_cols, 512)
    gp = pw - (3 * bw + 2 * wl + 2 * al)

    w_in_bf = w_in.astype(BF16)
    w_in_a = w_in_bf[:, :a_cols]
    w_in_b = jnp.pad(w_in_bf[:, a_cols:], ((0, 0), (0, pw - b_cols)))

    mu_p = jnp.pad(mu, ((0, 0), (0, pw - b_cols)))
    zeros_w = jnp.zeros((wl, bw), F32)
    zeros_a = jnp.zeros((al, bw), F32)
    w2cat = jnp.concatenate([jnp.concatenate([w2[0], zeros_w], 1), jnp.concatenate([zeros_w, w2[1]], 1)], 0)
    a2cat = jnp.concatenate([jnp.concatenate([a2[0], zeros_a], 1), jnp.concatenate([zeros_a, a2[1]], 1)], 0)
    g2p = jnp.pad(g2, ((0, gp - gl), (0, 0)))
    head_id = jnp.arange(min(bw, MXU_WIDTH)) // HEAD_DIM
    e = (head_id[:, None] == head_id[None, :]).astype(BF16)
    consts = (mu_p, w0.reshape(1, 2 * bw), w2cat.astype(BF16), a0.reshape(1, 2 * bw), a2cat.astype(BF16),
              g2p.astype(BF16), k_k.reshape(1, bw), k_a.reshape(1, bw), r_k.reshape(1, bw), e)

    b_full = jnp.repeat(b_s.T, aw // ws.shape[0], axis=1)
    ya = _gmlp_branch(xs, g, mx[3], mx[4], w_in_a, v_gain, ws.astype(BF16), b_full, rows_per_mod=seq)

    fx = _rwkv_front(xs, g, mx[3], mx[4], w_in_b, seq, consts, rows_per_mod=seq)
    fc = _rwkv_front(cs, g, mc[3], mc[4], w_in_b, ctx_len, consts, rows_per_mod=m_c)
    r_x, v_x, kk_x, g_x, bonus_x, lw_x, kka_x, kd_x = fx
    r_c, v_c, kk_c, _, _, lw_c, kka_c, kd_c = fc
    s_zero = jnp.zeros((bn, bw // LANE, LANE, LANE), F32)
    ys = []
    for reverse in (False, True):
        _, s_ctx = _chunk_scan(r_c, v_c, kk_c, lw_c, kka_c, kd_c, s_zero, bn=bn, reverse=reverse)
        y_dir, _ = _chunk_scan(r_x, v_x, kk_x, lw_x, kka_x, kd_x, s_ctx, bn=bn, reverse=reverse)
        ys.append(y_dir)

    w_out_bf = w_out.astype(BF16)
    return _mix_out(xs, mx[5], ya, ys[0], ys[1], bonus_x, g_x, ln_g.reshape(1, bw), ln_b.reshape(1, bw), e,
                    w_out_bf[:aw], w_out_bf[aw:], rows_per_mod=seq)
```
